```python
import jax, jax.numpy as jnp
from jax import lax
import numpy as np

D_MODEL = 2048
BATCH = 1
SEQ = 16384
DEPTH = 2
DEC_BATCH = 2
DEC_SEQ = 4096
PAST_LEN = 128

N_META = 16
N_HEADS = 8
NOPE_DIM = 128
ROPE_DIM = 64
V_DIM = 128
QK_DIM = NOPE_DIM + ROPE_DIM
Q_LORA = 512
KV_LORA = 256
D_ATTN = N_HEADS * V_DIM
POOL_WINDOWS = (2, 4, 8, 16)
D_POOL = D_MODEL - D_ATTN
POOL_GROUP = D_POOL // len(POOL_WINDOWS)
D_MIX = D_ATTN + D_POOL
D_IN = Q_LORA + KV_LORA + ROPE_DIM + D_POOL
D_FF = 5632
N_EXPERTS = 8
TOP_K = 2
D_FF_EXPERT = 7168
N_DENSE = (DEPTH + 1) // 2
N_MOE = DEPTH // 2
ROPE_THETA = 10000.0
RMS_EPS = 1e-6
Q_BLOCK = 128
EXPERT_BLOCK = 512

kernel_name = "hymba_mla_pool_moe_encoder"


def rmsnorm(x, g):
    xf = x.astype(jnp.float32)
    y = xf * lax.rsqrt(jnp.mean(xf * xf, axis=-1, keepdims=True) + RMS_EPS)
    return (y * g.astype(jnp.float32)).astype(x.dtype)


def rope_tables(n_pos):
    inv = ROPE_THETA ** (-jnp.arange(0, ROPE_DIM, 2, dtype=jnp.float32) / ROPE_DIM)
    ang = jnp.arange(n_pos, dtype=jnp.float32)[:, None] * inv[None, :]
    return jnp.cos(ang), jnp.sin(ang)


def apply_rope(x, cos, sin):
    xf = x.astype(jnp.float32)
    x1, x2 = xf[..., : ROPE_DIM // 2], xf[..., ROPE_DIM // 2:]
    out = jnp.concatenate([x1 * cos - x2 * sin, x2 * cos + x1 * sin], axis=-1)
    return out.astype(x.dtype)


def mla_attention(q_nope, q_rope, k_nope, k_rope, v):
    B, S, H, _ = q_nope.shape
    nqb = -(-S // Q_BLOCK)
    pad = nqb * Q_BLOCK - S
    scale = QK_DIM ** -0.5

    def blocks(q):
        q = jnp.pad(q, ((0, 0), (0, pad), (0, 0), (0, 0)))
        return q.reshape(B, nqb, Q_BLOCK, H, q.shape[-1]).transpose(1, 0, 2, 3, 4)

    def one_block(args):
        qn_b, qr_b = args
        s = (jnp.einsum('bqhd,bkhd->bhqk', qn_b, k_nope)
             + jnp.einsum('bqhr,bkr->bhqk', qr_b, k_rope))
        p = jax.nn.softmax(s.astype(jnp.float32) * scale, axis=-1).astype(v.dtype)
        return jnp.einsum('bhqk,bkhd->bqhd', p, v)

    o = lax.map(one_block, (blocks(q_nope), blocks(q_rope)))
    o = o.transpose(1, 0, 2, 3, 4).reshape(B, nqb * Q_BLOCK, H, V_DIM)[:, :S]
    return o.reshape(B, S, H * V_DIM)


def pool_mixer(u, pool_w, pool_scale):
    B, S, _ = u.shape
    cs = jnp.cumsum(u.astype(jnp.float32), axis=1)
    cs = jnp.pad(cs, ((0, 0), (1, 0), (0, 0)))
    t = jnp.arange(S)
    outs = []
    for g, w in enumerate(POOL_WINDOWS):
        lo = jnp.clip(t - w // 2, 0, S)
        hi = jnp.clip(t + w // 2, 0, S)
        sl = slice(g * POOL_GROUP, (g + 1) * POOL_GROUP)
        csg = cs[..., sl]
        cnt = (hi - lo).astype(jnp.float32)[:, None]
        mean = (jnp.take(csg, hi, axis=1) - jnp.take(csg, lo, axis=1)) / cnt
        pooled = (mean - u[..., sl].astype(jnp.float32)).astype(u.dtype)
        outs.append(pooled @ pool_w[g])
    return jnp.concatenate(outs, axis=-1) * pool_scale


def token_mixer(h, cos, sin, w_in, q_norm, w_uq, kv_norm, w_ukv, pool_w, pool_scale, w_out):
    B, S, _ = h.shape
    z = h @ w_in
    cq, ckv, kr, u = jnp.split(z, [Q_LORA, Q_LORA + KV_LORA, Q_LORA + KV_LORA + ROPE_DIM], axis=-1)
    q = (rmsnorm(cq, q_norm) @ w_uq).reshape(B, S, N_HEADS, QK_DIM)
    q_nope = q[..., :NOPE_DIM]
    q_rope = apply_rope(q[..., NOPE_DIM:], cos[:, None, :], sin[:, None, :])
    kv = (rmsnorm(ckv, kv_norm) @ w_ukv).reshape(B, S, N_HEADS, NOPE_DIM + V_DIM)
    k_nope, v = kv[..., :NOPE_DIM], kv[..., NOPE_DIM:]
    k_rope = apply_rope(kr, cos, sin)
    attn = mla_attention(q_nope, q_rope, k_nope, k_rope, v)
    pool = pool_mixer(u, pool_w, pool_scale)
    return jnp.concatenate([attn, pool], axis=-1) @ w_out


def dense_swiglu(h, w_gate, w_up, w_down):
    return (jax.nn.silu(h @ w_gate) * (h @ w_up)) @ w_down


def moe_swiglu(h, w_router, w_gate, w_up, w_down):
    B, S, D = h.shape
    T = B * S
    A = T * TOP_K
    xt = h.reshape(T, D)
    logits = (xt @ w_router).astype(jnp.float32)
    top_l, top_e = lax.top_k(logits, TOP_K)
    gates = jax.nn.softmax(top_l, axis=-1).astype(h.dtype)
    e_flat = top_e.reshape(-1)
    tok_flat = jnp.repeat(jnp.arange(T, dtype=jnp.int32), TOP_K)
    g_flat = gates.reshape(-1)
    order = jnp.argsort(e_flat)
    e_s, tok_s, g_s = e_flat[order], tok_flat[order], g_flat[order]
    counts = jnp.zeros((N_EXPERTS,), jnp.int32).at[e_flat].add(1)
    padded = (counts + EXPERT_BLOCK - 1) // EXPERT_BLOCK * EXPERT_BLOCK
    pad_end = jnp.cumsum(padded)
    pad_start = pad_end - padded
    grp_start = jnp.cumsum(counts) - counts
    dest = pad_start[e_s] + (jnp.arange(A, dtype=jnp.int32) - grp_start[e_s])
    n_blocks = -(-A // EXPERT_BLOCK) + N_EXPERTS
    buf = jnp.zeros((n_blocks * EXPERT_BLOCK, D), h.dtype).at[dest].set(xt[tok_s])
    blk_e = jnp.minimum(
        jnp.searchsorted(pad_end, jnp.arange(n_blocks, dtype=jnp.int32) * EXPERT_BLOCK, side='right'),
        N_EXPERTS - 1)

    def expert_block(args):
        xb, e = args
        return (jax.nn.silu(xb @ w_gate[e]) * (xb @ w_up[e])) @ w_down[e]

    yb = lax.map(expert_block, (buf.reshape(n_blocks, EXPERT_BLOCK, D), blk_e))
    y_s = yb.reshape(n_blocks * EXPERT_BLOCK, D)[dest] * g_s[:, None]
    out = jnp.zeros((T, D), h.dtype).at[tok_s].add(y_s)
    return out.reshape(B, S, D)


def trunk(x, meta_tokens, mix_norm, w_in, q_norm, w_uq, kv_norm, w_ukv, pool_w, pool_scale, w_out,
          ffn_norm, dense_w_gate, dense_w_up, dense_w_down,
          moe_w_router, moe_w_gate, moe_w_up, moe_w_down, final_norm):
    B = x.shape[0]
    meta = jnp.broadcast_to(meta_tokens.astype(x.dtype)[None], (B, N_META, D_MODEL))
    x = jnp.concatenate([meta, x], axis=1)
    cos, sin = rope_tables(x.shape[1])
    for l in range(DEPTH):
        h = rmsnorm(x, mix_norm[l])
        x = x + token_mixer(h, cos, sin, w_in[l], q_norm[l], w_uq[l], kv_norm[l], w_ukv[l],
                            pool_w[l], pool_scale[l], w_out[l])
        h = rmsnorm(x, ffn_norm[l])
        if l % 2 == 0:
            i = l // 2
            x = x + dense_swiglu(h, dense_w_gate[i], dense_w_up[i], dense_w_down[i])
        else:
            i = l // 2
            x = x + moe_swiglu(h, moe_w_router[i], moe_w_gate[i], moe_w_up[i], moe_w_down[i])
    return rmsnorm(x, final_norm)[:, N_META:]


def setup_inputs(seed: int = 0) -> dict:
    key = jax.random.key(seed)
    ks = jax.random.split(key, 24)
    f32 = jnp.float32

    def nrm(k, shape, fan_in):
        return jax.random.normal(k, shape, f32) * (fan_in ** -0.5)

    def gain(k, shape):
        return 1.0 + 0.02 * jax.random.normal(k, shape, f32)

    return {
        "x_prompt": jax.random.normal(ks[0], (BATCH, SEQ, D_MODEL), f32),
        "x_sample": jax.random.normal(ks[1], (DEC_BATCH, DEC_SEQ, D_MODEL), f32),
        "meta_tokens": jax.random.normal(ks[2], (N_META, D_MODEL), f32),
        "mix_norm": gain(ks[3], (DEPTH, D_MODEL)),
        "w_in": nrm(ks[4], (DEPTH, D_MODEL, D_IN), D_MODEL),
        "q_norm": gain(ks[5], (DEPTH, Q_LORA)),
        "w_uq": nrm(ks[6], (DEPTH, Q_LORA, N_HEADS * QK_DIM), Q_LORA),
        "kv_norm": gain(ks[7], (DEPTH, KV_LORA)),
        "w_ukv": nrm(ks[8], (DEPTH, KV_LORA, N_HEADS * (NOPE_DIM + V_DIM)), KV_LORA),
        "pool_w": nrm(ks[9], (DEPTH, len(POOL_WINDOWS), POOL_GROUP, POOL_GROUP), POOL_GROUP),
        "pool_scale": gain(ks[10], (DEPTH, D_POOL)),
        "w_out": nrm(ks[11], (DEPTH, D_MIX, D_MODEL), D_MIX),
        "ffn_norm": gain(ks[12], (DEPTH, D_MODEL)),
        "dense_w_gate": nrm(ks[13], (N_DENSE, D_MODEL, D_FF), D_MODEL),
        "dense_w_up": nrm(ks[14], (N_DENSE, D_MODEL, D_FF), D_MODEL),
        "dense_w_down": nrm(ks[15], (N_DENSE, D_FF, D_MODEL), D_FF),
        "moe_w_router": nrm(ks[16], (N_MOE, D_MODEL, N_EXPERTS), D_MODEL),
        "moe_w_gate": nrm(ks[17], (N_MOE, N_EXPERTS, D_MODEL, D_FF_EXPERT), D_MODEL),
        "moe_w_up": nrm(ks[18], (N_MOE, N_EXPERTS, D_MODEL, D_FF_EXPERT), D_MODEL),
        "moe_w_down": nrm(ks[19], (N_MOE, N_EXPERTS, D_FF_EXPERT, D_MODEL), D_FF_EXPERT),
        "final_norm": gain(ks[20], (D_MODEL,)),
    }


def reference(x_prompt, x_sample, meta_tokens, mix_norm, w_in, q_norm, w_uq, kv_norm, w_ukv,
              pool_w, pool_scale, w_out, ffn_norm, dense_w_gate, dense_w_up, dense_w_down,
              moe_w_router, moe_w_gate, moe_w_up, moe_w_down, final_norm):
    y_prompt = trunk(x_prompt, meta_tokens, mix_norm, w_in, q_norm, w_uq, kv_norm, w_ukv, pool_w,
                     pool_scale, w_out, ffn_norm, dense_w_gate, dense_w_up, dense_w_down,
                     moe_w_router, moe_w_gate, moe_w_up, moe_w_down, final_norm)
    y_sample = trunk(x_sample, meta_tokens, mix_norm, w_in, q_norm, w_uq, kv_norm, w_ukv, pool_w,
                     pool_scale, w_out, ffn_norm, dense_w_gate, dense_w_up, dense_w_down,
                     moe_w_router, moe_w_gate, moe_w_up, moe_w_down, final_norm)
    return (y_prompt, y_sample)
```

```python
import functools

import jax
import jax.numpy as jnp
from jax import lax
from jax.experimental import pallas as pl
from jax.experimental.pallas import tpu as pltpu

N_HEADS = 8
NOPE_DIM = 128
ROPE_DIM = 64
V_DIM = 128
QK_DIM = NOPE_DIM + ROPE_DIM
HEAD_PAD = 256
POOL_WINDOWS = (2, 4, 8, 16)
HALO = 8
TOP_K = 2
EXPERT_BLOCK = 512
ROPE_THETA = 10000.0
RMS_EPS = 1e-6
LOG2E = 1.4426950408889634

LANES = 128
ALIGN = 256
NEG = -1e30
VMEM_LIMIT = 56 * 1024 * 1024

F32 = jnp.float32
BF16 = jnp.bfloat16


def _rms(x, g):
    ms = jnp.mean(x * x, axis=-1, keepdims=True)
    return x * lax.rsqrt(ms + RMS_EPS) * g


def _dot(a, b):
    return jnp.dot(a, b, preferred_element_type=F32)


def _dot_nt(a, b):
    return lax.dot_general(a, b, (((1,), (1,)), ((), ())), preferred_element_type=F32)


def _params(sem):
    return pltpu.CompilerParams(dimension_semantics=sem, vmem_limit_bytes=VMEM_LIMIT)


def _largest_tile(n, candidates):
    for c in candidates:
        if n % c == 0:
            return c
    raise ValueError(f"no tile in {candidates} divides {n}")


def _in_kernel(x_ref, g_ref, win_ref, qn_ref, wq_ref, kvn_ref, wk_ref, wvt_ref, cos_ref, sin_ref,
               q_ref, k_ref, vt_ref, u_ref, *, ql, kvl, dp, scale):
    h = _rms(x_ref[...], g_ref[...]).astype(BF16)
    z = _dot(h, win_ref[...])
    cq = z[:, :ql]
    ckv = z[:, ql:ql + kvl]
    u_ref[...] = z[:, ql + kvl:ql + kvl + dp]
    kr = z[:, ql + kvl + dp:]
    cqn = _rms(cq, qn_ref[...]).astype(BF16)
    ckvn = _rms(ckv, kvn_ref[...]).astype(BF16)
    q = _dot(cqn, wq_ref[...])
    kn = _dot(ckvn, wk_ref[...])
    vt_ref[...] = _dot_nt(wvt_ref[...], ckvn).astype(BF16)

    c = cos_ref[...]
    s = sin_ref[...]
    first_half = lax.broadcasted_iota(jnp.int32, c.shape, 1) < ROPE_DIM // 2

    def rope(v):
        partner = jnp.where(first_half, pltpu.roll(v, LANES - ROPE_DIM // 2, 1),
                            pltpu.roll(v, ROPE_DIM // 2, 1))
        return v * c + partner * s

    kr_r = rope(kr).astype(BF16)
    for hh in range(N_HEADS):
        lo = hh * HEAD_PAD
        q_ref[:, lo:lo + NOPE_DIM] = (q[:, lo:lo + NOPE_DIM] * scale).astype(BF16)
        q_ref[:, lo + NOPE_DIM:lo + HEAD_PAD] = (rope(q[:, lo + NOPE_DIM:lo + HEAD_PAD]) * scale).astype(BF16)
        k_ref[:, lo:lo + NOPE_DIM] = kn[:, hh * NOPE_DIM:(hh + 1) * NOPE_DIM].astype(BF16)
        k_ref[:, lo + NOPE_DIM:lo + HEAD_PAD] = kr_r


def _in_proj(x, tile_off, n_tiles, tiles_per_seq, g, win, qn, wq, kvn, wk, wvt, cos_t, sin_t, dims):
    d, ql, kvl, dp = dims
    t = ALIGN
    rows = n_tiles * t
    zw = win.shape[1]
    const = lambda i: (0, 0)
    kern = functools.partial(_in_kernel, ql=ql, kvl=kvl, dp=dp, scale=QK_DIM ** -0.5 * LOG2E)
    return pl.pallas_call(
        kern,
        grid=(n_tiles,),
        in_specs=[
            pl.BlockSpec((t, d), lambda i: (i + tile_off, 0)),
            pl.BlockSpec((1, d), const),
            pl.BlockSpec((d, zw), const),
            pl.BlockSpec((1, ql), const),
            pl.BlockSpec((ql, N_HEADS * HEAD_PAD), const),
            pl.BlockSpec((1, kvl), const),
            pl.BlockSpec((kvl, N_HEADS * NOPE_DIM), const),
            pl.BlockSpec((N_HEADS * V_DIM, kvl), const),
            pl.BlockSpec((t, LANES), lambda i: (i % tiles_per_seq, 0)),
            pl.BlockSpec((t, LANES), lambda i: (i % tiles_per_seq, 0)),
        ],
        out_specs=[
            pl.BlockSpec((t, N_HEADS * HEAD_PAD), lambda i: (i, 0)),
            pl.BlockSpec((t, N_HEADS * HEAD_PAD), lambda i: (i, 0)),
            pl.BlockSpec((N_HEADS * V_DIM, t), lambda i: (0, i)),
            pl.BlockSpec((t, dp), lambda i: (i, 0)),
        ],
        out_shape=[
            jax.ShapeDtypeStruct((rows, N_HEADS * HEAD_PAD), BF16),
            jax.ShapeDtypeStruct((rows, N_HEADS * HEAD_PAD), BF16),
            jax.ShapeDtypeStruct((N_HEADS * V_DIM, rows), BF16),
            jax.ShapeDtypeStruct((rows, dp), F32),
        ],
        compiler_params=_params(("arbitrary",)),
        name="in_proj",
    )(x, g, win, qn, wq, kvn, wk, wvt, cos_t, sin_t)


def _attn_kernel(q_ref, k_ref, vt_ref, o_ref, *, s_len, tk, first_valid):
    q = q_ref[...]
    tq = q.shape[0]

    def step(kb, vtb, carry, mask=None):
        m, l, acc = carry
        s = _dot_nt(kb, q)
        if mask is not None:
            s = jnp.where(mask, s, NEG)
        m_new = jnp.maximum(m, jnp.max(s, axis=0, keepdims=True))
        p = jnp.exp2(s - m_new)
        alpha = jnp.exp2(m - m_new)
        l = alpha * l + jnp.sum(p, axis=0, keepdims=True)
        acc = alpha * acc + _dot(vtb, p.astype(BF16))
        return m_new, l, acc

    carry = (jnp.full((1, tq), NEG, F32), jnp.zeros((1, tq), F32), jnp.zeros((V_DIM, tq), F32))
    lo = ALIGN - LANES
    row = lax.broadcasted_iota(jnp.int32, (LANES, tq), 0) + lo
    carry = step(k_ref[lo:ALIGN, :], vt_ref[:, lo:ALIGN], carry, mask=row >= first_valid)

    def body(c, carry):
        off = pl.multiple_of(ALIGN + c * tk, LANES)
        return step(k_ref[pl.ds(off, tk), :], vt_ref[:, pl.ds(off, tk)], carry)

    m, l, acc = lax.fori_loop(0, s_len // tk, body, carry)
    o_ref[...] = (acc / l).T.astype(BF16)


def _attention(q, k, vt, n_seq, s_len, first_valid):
    region = ALIGN + s_len
    tq = ALIGN
    nq = region // tq
    tk = _largest_tile(s_len, (2048, 1024, 512, 256))
    kern = functools.partial(_attn_kernel, s_len=s_len, tk=tk, first_valid=first_valid)
    return pl.pallas_call(
        kern,
        grid=(n_seq, N_HEADS, nq),
        in_specs=[
            pl.BlockSpec((tq, HEAD_PAD), lambda s, h, i: (s * nq + i, h)),
            pl.BlockSpec((region, HEAD_PAD), lambda s, h, i: (s, h)),
            pl.BlockSpec((V_DIM, region), lambda s, h, i: (h, s)),
        ],
        out_specs=pl.BlockSpec((tq, V_DIM), lambda s, h, i: (s * nq + i, h)),
        out_shape=jax.ShapeDtypeStruct((n_seq * region, N_HEADS * V_DIM), BF16),
        compiler_params=_params(("arbitrary", "arbitrary", "arbitrary")),
        name="attention",
    )(q, k, vt)


def _out_kernel(attn_ref, u_ref, uprev_ref, unext_ref, x_ref, pw_ref, ps_ref, woa_ref, wop_ref, o_ref,
                *, tiles_per_seq, first_valid, n_pos):
    t, dp = u_ref.shape
    pg = dp // len(POOL_WINDOWS)
    p0 = (pl.program_id(0) % tiles_per_seq) * t - first_valid
    u = u_ref[...]
    ucat = jnp.concatenate([uprev_ref[...], u, unext_ref[...]], axis=0).astype(BF16)
    tt = lax.broadcasted_iota(jnp.int32, (t, t + 2 * HALO), 0)
    jj = lax.broadcasted_iota(jnp.int32, (t, t + 2 * HALO), 1)
    rel = jj - HALO - tt
    pj = p0 - HALO + jj
    col_ok = (pj >= 0) & (pj < n_pos)
    pos = p0 + lax.broadcasted_iota(jnp.int32, (t, 1), 0)
    ps = ps_ref[...]
    pooled = []
    for gi, w in enumerate(POOL_WINDOWS):
        band = jnp.where((rel >= -(w // 2)) & (rel <= w // 2 - 1) & col_ok, 1.0, 0.0).astype(BF16)
        wsum = _dot(band, ucat[:, gi * pg:(gi + 1) * pg])
        cnt = jnp.minimum(pos + w // 2, n_pos) - jnp.maximum(pos - w // 2, 0)
        cnt = jnp.maximum(cnt, 1).astype(F32)
        centred = (wsum / cnt - u[:, gi * pg:(gi + 1) * pg]).astype(BF16)
        pooled.append(_dot(centred, pw_ref[gi]) * ps[:, gi * pg:(gi + 1) * pg])
    pool = jnp.concatenate(pooled, axis=1).astype(BF16)
    y = _dot(attn_ref[...], woa_ref[...]) + _dot(pool, wop_ref[...])
    o_ref[...] = x_ref[...] + y


def _out_proj(x, attn, u, tile_off, n_tiles, tiles_per_seq, pw, ps, woa, wop, dims, first_valid, n_pos):
    d, _, _, dp = dims
    t = ALIGN
    pg = dp // len(POOL_WINDOWS)
    da = N_HEADS * V_DIM
    hb = t // HALO
    last = n_tiles * hb - 1
    const = lambda i: (0, 0)
    kern = functools.partial(_out_kernel, tiles_per_seq=tiles_per_seq, first_valid=first_valid, n_pos=n_pos)
    return pl.pallas_call(
        kern,
        grid=(n_tiles,),
        in_specs=[
            pl.BlockSpec((t, da), lambda i: (i, 0)),
            pl.BlockSpec((t, dp), lambda i: (i, 0)),
            pl.BlockSpec((HALO, dp), lambda i: (jnp.maximum(i * hb - 1, 0), 0)),
            pl.BlockSpec((HALO, dp), lambda i: (jnp.minimum((i + 1) * hb, last), 0)),
            pl.BlockSpec((t, d), lambda i: (i + tile_off, 0)),
            pl.BlockSpec((len(POOL_WINDOWS), pg, pg), lambda i: (0, 0, 0)),
            pl.BlockSpec((1, dp), const),
            pl.BlockSpec((da, d), const),
            pl.BlockSpec((dp, d), const),
        ],
        out_specs=pl.BlockSpec((t, d), lambda i: (i + tile_off, 0)),
        out_shape=jax.ShapeDtypeStruct(x.shape, F32),
        input_output_aliases={4: 0},
        compiler_params=_params(("arbitrary",)),
        name="out_proj",
    )(attn, u, u, u, x, pw, ps, woa, wop)


def _swiglu(h, wg, wu, wd):
    a = _dot(h, wg)
    b = _dot(h, wu)
    act = a * (1.0 / (1.0 + jnp.exp(-a))) * b
    return _dot(act.astype(BF16), wd)


def _ffn_kernel(x_ref, g_ref, wg_ref, wu_ref, wd_ref, o_ref, h_ref):
    @pl.when(pl.program_id(1) == 0)
    def _():
        x = x_ref[...]
        h_ref[...] = _rms(x, g_ref[...]).astype(BF16)
        o_ref[...] = x

    o_ref[...] += _swiglu(h_ref[...], wg_ref[...], wu_ref[...], wd_ref[...])


def _dense_ffn(x, g, wg, wu, wd):
    n, d = x.shape
    dff = wg.shape[1]
    t = _largest_tile(n, (768, 512, 256))
    tf = _largest_tile(dff, (512, 256, 128))
    return pl.pallas_call(
        _ffn_kernel,
        grid=(n // t, dff // tf),
        in_specs=[
            pl.BlockSpec((t, d), lambda i, j: (i, 0)),
            pl.BlockSpec((1, d), lambda i, j: (0, 0)),
            pl.BlockSpec((d, tf), lambda i, j: (0, j)),
            pl.BlockSpec((d, tf), lambda i, j: (0, j)),
            pl.BlockSpec((tf, d), lambda i, j: (j, 0)),
        ],
        out_specs=pl.BlockSpec((t, d), lambda i, j: (i, 0)),
        out_shape=jax.ShapeDtypeStruct((n, d), F32),
        scratch_shapes=[pltpu.VMEM((t, d), BF16)],
        compiler_params=_params(("arbitrary", "arbitrary")),
        name="dense_ffn",
    )(x, g, wg, wu, wd)


def _router_kernel(x_ref, g_ref, wr_ref, e_ref, p_ref, *, n_exp):
    h = _rms(x_ref[...], g_ref[...])
    logits = jnp.dot(h, wr_ref[...], preferred_element_type=F32, precision=lax.Precision.HIGHEST)
    lane = lax.broadcasted_iota(jnp.int32, logits.shape, 1)
    logits = jnp.where(lane < n_exp, logits, -jnp.inf)
    lane_f = lane.astype(F32)
    m1 = jnp.max(logits, axis=1, keepdims=True)
    i1 = jnp.min(jnp.where(logits == m1, lane_f, float(LANES)), axis=1, keepdims=True)
    rest = jnp.where(lane_f == i1, -jnp.inf, logits)
    m2 = jnp.max(rest, axis=1, keepdims=True)
    i2 = jnp.min(jnp.where(rest == m2, lane_f, float(LANES)), axis=1, keepdims=True)
    e2 = jnp.exp(m2 - m1)
    den = 1.0 + e2
    e_ref[...] = jnp.where(lane == 0, i1, jnp.where(lane == 1, i2, 0.0)).astype(jnp.int32)
    p_ref[...] = jnp.where(lane == 0, 1.0 / den, jnp.where(lane == 1, e2 / den, 0.0))


def _router(x, g, wr, n_exp):
    n, d = x.shape
    t = _largest_tile(n, (768, 512, 256))
    return pl.pallas_call(
        functools.partial(_router_kernel, n_exp=n_exp),
        grid=(n // t,),
        in_specs=[
            pl.BlockSpec((t, d), lambda i: (i, 0)),
            pl.BlockSpec((1, d), lambda i: (0, 0)),
            pl.BlockSpec((d, LANES), lambda i: (0, 0)),
        ],
        out_specs=[pl.BlockSpec((t, LANES), lambda i: (i, 0)), pl.BlockSpec((t, LANES), lambda i: (i, 0))],
        out_shape=[jax.ShapeDtypeStruct((n, LANES), jnp.int32), jax.ShapeDtypeStruct((n, LANES), F32)],
        compiler_params=_params(("arbitrary",)),
        name="router",
    )(x, g, wr)


def _gather_rows(idx_ref, base, src_hbm, dst_ref, sem, n_rows):
    def copy(r):
        return pltpu.make_async_copy(src_hbm.at[pl.ds(idx_ref[base + r], 1), :], dst_ref.at[pl.ds(r, 1), :], sem)

    def start(r, _):
        copy(r).start()
        return 0

    def wait(r, _):
        copy(r).wait()
        return 0

    lax.fori_loop(0, n_rows, start, 0)
    lax.fori_loop(0, n_rows, wait, 0)


def _gather_kernel(idx_ref, x_hbm, o_ref, sem):
    rows = o_ref.shape[0]
    _gather_rows(idx_ref, pl.program_id(0) * rows, x_hbm, o_ref, sem, rows)


def _gather(x, idx):
    n_rows = idx.shape[0]
    d = x.shape[1]
    r = EXPERT_BLOCK
    return pl.pallas_call(
        _gather_kernel,
        grid_spec=pltpu.PrefetchScalarGridSpec(
            num_scalar_prefetch=1,
            grid=(n_rows // r,),
            in_specs=[pl.BlockSpec(memory_space=pl.ANY)],
            out_specs=pl.BlockSpec((r, d), lambda b, idx: (b, 0)),
            scratch_shapes=[pltpu.SemaphoreType.DMA],
        ),
        out_shape=jax.ShapeDtypeStruct((n_rows, d), x.dtype),
        compiler_params=_params(("arbitrary",)),
        name="moe_gather",
    )(idx, x)


def _moe_kernel(blk_e_ref, n_act_ref, x_ref, g_ref, wg_ref, wu_ref, wd_ref, o_ref, h_ref):
    b = pl.program_id(0)
    j = pl.program_id(1)
    active = b < n_act_ref[0]

    @pl.when(j == 0)
    def _():
        h_ref[...] = _rms(x_ref[...], g_ref[...]).astype(BF16)
        o_ref[...] = jnp.zeros_like(o_ref)

    @pl.when(active)
    def _():
        o_ref[...] += _swiglu(h_ref[...], wg_ref[0], wu_ref[0], wd_ref[0])


def _moe_experts(buf, g, wg, wu, wd, blk_e, n_act):
    n_rows, d = buf.shape
    dff = wg.shape[2]
    r = EXPERT_BLOCK
    tf = _largest_tile(dff, (1024, 512, 256, 128))
    nj = dff // tf

    def wmap_in(b, j, blk_e, n_act):
        return (blk_e[b], 0, jnp.where(b < n_act[0], j, nj - 1))

    def wmap_dn(b, j, blk_e, n_act):
        return (blk_e[b], jnp.where(b < n_act[0], j, nj - 1), 0)

    return pl.pallas_call(
        _moe_kernel,
        grid_spec=pltpu.PrefetchScalarGridSpec(
            num_scalar_prefetch=2,
            grid=(n_rows // r, nj),
            in_specs=[
                pl.BlockSpec((r, d), lambda b, j, *_: (b, 0)),
                pl.BlockSpec((1, d), lambda b, j, *_: (0, 0)),
                pl.BlockSpec((1, d, tf), wmap_in),
                pl.BlockSpec((1, d, tf), wmap_in),
                pl.BlockSpec((1, tf, d), wmap_dn),
            ],
            out_specs=pl.BlockSpec((r, d), lambda b, j, *_: (b, 0)),
            scratch_shapes=[pltpu.VMEM((r, d), BF16)],
        ),
        out_shape=jax.ShapeDtypeStruct((n_rows, d), F32),
        compiler_params=_params(("arbitrary", "arbitrary")),
        name="moe_experts",
    )(blk_e, n_act, buf, g, wg, wu, wd)


def _combine_kernel(pos_ref, x_ref, p_ref, fn_ref, y_hbm, o_ref, y0_ref, y1_ref, sem, *, tile_of, final_norm):
    t = x_ref.shape[0]
    base = tile_of(pl.program_id(0)) * t * TOP_K

    def copies(r):
        c0 = pltpu.make_async_copy(y_hbm.at[pl.ds(pos_ref[base + TOP_K * r], 1), :], y0_ref.at[pl.ds(r, 1), :], sem)
        c1 = pltpu.make_async_copy(y_hbm.at[pl.ds(pos_ref[base + TOP_K * r + 1], 1), :], y1_ref.at[pl.ds(r, 1), :], sem)
        return c0, c1

    def start(r, _):
        for c in copies(r):
            c.start()
        return 0

    def wait(r, _):
        for c in copies(r):
            c.wait()
        return 0

    lax.fori_loop(0, t, start, 0)
    lax.fori_loop(0, t, wait, 0)
    p = p_ref[...]
    y = x_ref[...] + (y0_ref[...] * p[:, 0:1] + y1_ref[...] * p[:, 1:2])
    if final_norm:
        y = _rms(y, fn_ref[...])
    o_ref[...] = y


def _combine(x, gates, pos, yb, fn, n_out_tiles, tile_of, final_norm):
    d = x.shape[1]
    t = ALIGN
    kern = functools.partial(_combine_kernel, tile_of=tile_of, final_norm=final_norm)
    return pl.pallas_call(
        kern,
        grid_spec=pltpu.PrefetchScalarGridSpec(
            num_scalar_prefetch=1,
            grid=(n_out_tiles,),
            in_specs=[
                pl.BlockSpec((t, d), lambda i, pos: (tile_of(i), 0)),
                pl.BlockSpec((t, LANES), lambda i, pos: (tile_of(i), 0)),
                pl.BlockSpec((1, d), lambda i, pos: (0, 0)),
                pl.BlockSpec(memory_space=pl.ANY),
            ],
            out_specs=pl.BlockSpec((t, d), lambda i, pos: (i, 0)),
            scratch_shapes=[pltpu.VMEM((t, d), F32), pltpu.VMEM((t, d), F32), pltpu.SemaphoreType.DMA],
        ),
        out_shape=jax.ShapeDtypeStruct((n_out_tiles * t, d), F32),
        compiler_params=_params(("arbitrary",)),
        name="moe_combine",
    )(pos, x, gates, fn, yb)


def _norm_kernel(x_ref, g_ref, o_ref):
    o_ref[...] = _rms(x_ref[...], g_ref[...])


def _final_norm(x, g, n_out_tiles, tile_of):
    d = x.shape[1]
    t = ALIGN
    return pl.pallas_call(
        _norm_kernel,
        grid=(n_out_tiles,),
        in_specs=[pl.BlockSpec((t, d), lambda i: (tile_of(i), 0)), pl.BlockSpec((1, d), lambda i: (0, 0))],
        out_specs=pl.BlockSpec((t, d), lambda i: (i, 0)),
        out_shape=jax.ShapeDtypeStruct((n_out_tiles * t, d), F32),
        compiler_params=_params(("arbitrary",)),
        name="final_norm",
    )(x, g)


def _routing_tables(top_e, valid, n_exp):
    n = top_e.shape[0]
    a = n * TOP_K
    e_flat = jnp.where(valid[:, None], top_e, n_exp).reshape(-1)
    onehot = (e_flat[:, None] == jnp.arange(n_exp, dtype=jnp.int32)[None, :]).astype(jnp.int32)
    csum = jnp.cumsum(onehot, axis=0)
    counts = csum[-1]
    rank = jnp.sum((csum - onehot) * onehot, axis=1)
    padded = (counts + EXPERT_BLOCK - 1) // EXPERT_BLOCK * EXPERT_BLOCK
    pad_end = jnp.cumsum(padded)
    pad_start = pad_end - padded
    n_blocks = -(-a // EXPERT_BLOCK) + n_exp
    n_rows = n_blocks * EXPERT_BLOCK
    e_safe = jnp.minimum(e_flat, n_exp - 1)
    routed = e_flat < n_exp
    dest = jnp.where(routed, pad_start[e_safe] + rank, n_rows)
    tok = jnp.arange(a, dtype=jnp.int32) // TOP_K
    src_tok = jnp.zeros((n_rows,), jnp.int32).at[dest].set(tok, mode="drop")
    blk_e = jnp.minimum(
        jnp.searchsorted(pad_end, jnp.arange(n_blocks, dtype=jnp.int32) * EXPERT_BLOCK, side="right"),
        n_exp - 1).astype(jnp.int32)
    n_act = (pad_end[-1] // EXPERT_BLOCK).astype(jnp.int32).reshape(1)
    last_e = blk_e[jnp.maximum(n_act[0] - 1, 0)]
    blk_e = jnp.where(jnp.arange(n_blocks) < n_act[0], blk_e, last_e)
    pos = jnp.where(routed, dest, 0).astype(jnp.int32)
    return src_tok, blk_e, n_act, pos


def _rope_tables(n_rows, first_valid):
    inv = ROPE_THETA ** (-jnp.arange(0, ROPE_DIM, 2, dtype=F32) / ROPE_DIM)
    pos = (jnp.arange(n_rows, dtype=jnp.int32) - first_valid).astype(F32)
    ang = pos[:, None] * inv[None, :]
    cos, sin = jnp.cos(ang), jnp.sin(ang)
    zeros = jnp.zeros((n_rows, LANES - ROPE_DIM), F32)
    return jnp.concatenate([cos, cos, zeros], axis=1), jnp.concatenate([-sin, sin, zeros], axis=1)


def kernel(x_prompt, x_sample, meta_tokens, mix_norm, w_in, q_norm, w_uq, kv_norm, w_ukv, pool_w, pool_scale, w_out, ffn_norm, dense_w_gate, dense_w_up, dense_w_down, moe_w_router, moe_w_gate, moe_w_up, moe_w_down, final_norm):
    depth, d = mix_norm.shape
    n_meta = meta_tokens.shape[0]
    ql = q_norm.shape[1]
    kvl = kv_norm.shape[1]
    dp = pool_scale.shape[1]
    n_exp = moe_w_router.shape[2]
    da = N_HEADS * V_DIM
    assert d == da + dp and dp % (len(POOL_WINDOWS) * LANES) == 0
    assert ql % LANES == 0 and kvl % LANES == 0 and n_meta <= LANES and n_meta % 8 == 0
    assert w_in.shape[2] == ql + kvl + ROPE_DIM + dp and n_exp <= LANES
    dims = (d, ql, kvl, dp)
    first_valid = ALIGN - n_meta

    groups = [(x_prompt.shape[0], x_prompt.shape[1]), (x_sample.shape[0], x_sample.shape[1])]
    for _, s_len in groups:
        assert s_len % ALIGN == 0
    lead = jnp.concatenate([jnp.zeros((first_valid, d), F32), meta_tokens.astype(F32)], axis=0)
    parts = []
    for xs in (x_prompt, x_sample):
        for bi in range(xs.shape[0]):
            parts += [lead, xs[bi]]
    x = jnp.concatenate(parts, axis=0)
    n_tok = x.shape[0]
    valid = jnp.concatenate(
        [jnp.tile(jnp.arange(ALIGN + s_len) >= first_valid, nb) for nb, s_len in groups])

    tables = [_rope_tables(ALIGN + s_len, first_valid) for _, s_len in groups]

    for l in range(depth):
        wi = w_in[l]
        c0, c1, c2 = ql, ql + kvl, ql + kvl + ROPE_DIM
        win = jnp.concatenate(
            [wi[:, :c1], wi[:, c2:], wi[:, c1:c2], jnp.zeros((d, LANES - ROPE_DIM), F32)], axis=1).astype(BF16)
        wq3 = w_uq[l].reshape(ql, N_HEADS, QK_DIM)
        wq = jnp.concatenate([wq3, jnp.zeros((ql, N_HEADS, HEAD_PAD - QK_DIM), F32)], axis=2)
        wq = wq.reshape(ql, N_HEADS * HEAD_PAD).astype(BF16)
        wkv3 = w_ukv[l].reshape(kvl, N_HEADS, NOPE_DIM + V_DIM)
        wk = wkv3[:, :, :NOPE_DIM].reshape(kvl, N_HEADS * NOPE_DIM).astype(BF16)
        wvt = wkv3[:, :, NOPE_DIM:].reshape(kvl, N_HEADS * V_DIM).T.astype(BF16)
        woa = w_out[l][:da].astype(BF16)
        wop = w_out[l][da:].astype(BF16)
        pw = pool_w[l].astype(BF16)

        tile_off = 0
        for (nb, s_len), (cos_t, sin_t) in zip(groups, tables):
            tps = (ALIGN + s_len) // ALIGN
            n_tiles = nb * tps
            q, k, vt, u = _in_proj(x, tile_off, n_tiles, tps, mix_norm[l][None], win, q_norm[l][None], wq,
                                   kv_norm[l][None], wk, wvt, cos_t, sin_t, dims)
            attn = _attention(q, k, vt, nb, s_len, first_valid)
            x = _out_proj(x, attn, u, tile_off, n_tiles, tps, pw, pool_scale[l][None], woa, wop, dims,
                          first_valid, n_meta + s_len)
            tile_off += n_tiles

        last = l == depth - 1
        if l % 2 == 0:
            i = l // 2
            x = _dense_ffn(x, ffn_norm[l][None], dense_w_gate[i].astype(BF16), dense_w_up[i].astype(BF16),
                           dense_w_down[i].astype(BF16))
            moe = None
        else:
            i = l // 2
            wr = jnp.concatenate([moe_w_router[i], jnp.zeros((d, LANES - n_exp), F32)], axis=1)
            top_e, gates = _router(x, ffn_norm[l][None], wr, n_exp)
            src_tok, blk_e, n_act, pos = _routing_tables(top_e[:, :TOP_K], valid, n_exp)
            buf = _gather(x, src_tok)
            yb = _moe_experts(buf, ffn_norm[l][None], moe_w_gate[i].astype(BF16), moe_w_up[i].astype(BF16),
                              moe_w_down[i].astype(BF16), blk_e, n_act)
            moe = (gates, pos, yb)
            if not last:
                x = _combine(x, gates, pos, yb, final_norm[None], n_tok // ALIGN, lambda i: i, False)
                moe = None

    outs = []
    tile_off = 0
    for (nb, s_len), xs in zip(groups, (x_prompt, x_sample)):
        real = s_len // ALIGN

        def tile_of(i, tile_off=tile_off, real=real):
            return tile_off + (i // real) * (real + 1) + 1 + i % real

        if moe is not None:
            gates, pos, yb = moe
            y = _combine(x, gates, pos, yb, final_norm[None], nb * real, tile_of, True)
        else:
            y = _final_norm(x, final_norm[None], nb * real, tile_of)
        outs.append(y.reshape(nb, s_len, d).astype(xs.dtype))
        tile_off += nb * (real + 1)
    return tuple(outs)
```

```python
import functools

import jax
import jax.numpy as jnp
from jax import lax
from jax.experimental import pallas as pl
from jax.experimental.pallas import tpu as pltpu

N_HEADS = 8
NOPE_DIM = 128
ROPE_DIM = 64
V_DIM = 128
QK_DIM = NOPE_DIM + ROPE_DIM
HEAD_PAD = 256
POOL_WINDOWS = (2, 4, 8, 16)
HALO = 8
TOP_K = 2
EXPERT_BLOCK = 512
ROPE_THETA = 10000.0
RMS_EPS = 1e-6
LOG2E = 1.4426950408889634

LANES = 128
ALIGN = 256
MAX_Q_UNITS = 5
MAX_Q_UNITS_WHOLE = 17
NEG = -1e30
VMEM_LIMIT = 56 * 1024 * 1024

F32 = jnp.float32
BF16 = jnp.bfloat16


def _rms(x, g):
    ms = jnp.mean(x * x, axis=-1, keepdims=True)
    return x * lax.rsqrt(ms + RMS_EPS) * g


def _dot(a, b):
    return jnp.dot(a, b, preferred_element_type=F32)


def _dot_nt(a, b):
    return lax.dot_general(a, b, (((1,), (1,)), ((), ())), preferred_element_type=F32)


def _params(sem):
    return pltpu.CompilerParams(dimension_semantics=sem, vmem_limit_bytes=VMEM_LIMIT)


def _largest_tile(n, candidates):
    for c in candidates:
        if n % c == 0:
            return c
    raise ValueError(f"no tile in {candidates} divides {n}")


def _in_kernel(x_ref, g_ref, win_ref, qn_ref, wq_ref, kvn_ref, wk_ref, wvt_ref, cos_ref, sin_ref,
               q_ref, k_ref, vt_ref, u_ref, *, ql, kvl, dp, scale):
    h = _rms(x_ref[...], g_ref[...]).astype(BF16)
    z = _dot(h, win_ref[...])
    cq = z[:, :ql]
    ckv = z[:, ql:ql + kvl]
    u_ref[...] = z[:, ql + kvl:ql + kvl + dp]
    kr = z[:, ql + kvl + dp:]
    cqn = _rms(cq, qn_ref[...]).astype(BF16)
    ckvn = _rms(ckv, kvn_ref[...]).astype(BF16)
    q = _dot(cqn, wq_ref[...])
    kn = _dot(ckvn, wk_ref[...])
    vt_ref[...] = _dot_nt(wvt_ref[...], ckvn).astype(BF16)

    c = cos_ref[...]
    s = sin_ref[...]
    first_half = lax.broadcasted_iota(jnp.int32, c.shape, 1) < ROPE_DIM // 2

    def rope(v):
        partner = jnp.where(first_half, pltpu.roll(v, LANES - ROPE_DIM // 2, 1),
                            pltpu.roll(v, ROPE_DIM // 2, 1))
        return v * c + partner * s

    kr_r = rope(kr).astype(BF16)
    for hh in range(N_HEADS):
        lo = hh * HEAD_PAD
        q_ref[:, lo:lo + NOPE_DIM] = (q[:, lo:lo + NOPE_DIM] * scale).astype(BF16)
        q_ref[:, lo + NOPE_DIM:lo + HEAD_PAD] = (rope(q[:, lo + NOPE_DIM:lo + HEAD_PAD]) * scale).astype(BF16)
        k_ref[:, lo:lo + NOPE_DIM] = kn[:, hh * NOPE_DIM:(hh + 1) * NOPE_DIM].astype(BF16)
        k_ref[:, lo + NOPE_DIM:lo + HEAD_PAD] = kr_r


def _in_proj(x, tile_off, n_tiles, tiles_per_seq, g, win, qn, wq, kvn, wk, wvt, cos_t, sin_t, dims):
    d, ql, kvl, dp = dims
    t = ALIGN
    rows = n_tiles * t
    zw = win.shape[1]
    const = lambda i: (0, 0)
    kern = functools.partial(_in_kernel, ql=ql, kvl=kvl, dp=dp, scale=QK_DIM ** -0.5 * LOG2E)
    return pl.pallas_call(
        kern,
        grid=(n_tiles,),
        in_specs=[
            pl.BlockSpec((t, d), lambda i: (i + tile_off, 0)),
            pl.BlockSpec((1, d), const),
            pl.BlockSpec((d, zw), const),
            pl.BlockSpec((1, ql), const),
            pl.BlockSpec((ql, N_HEADS * HEAD_PAD), const),
            pl.BlockSpec((1, kvl), const),
            pl.BlockSpec((kvl, N_HEADS * NOPE_DIM), const),
            pl.BlockSpec((N_HEADS * V_DIM, kvl), const),
            pl.BlockSpec((t, LANES), lambda i: (i % tiles_per_seq, 0)),
            pl.BlockSpec((t, LANES), lambda i: (i % tiles_per_seq, 0)),
        ],
        out_specs=[
            pl.BlockSpec((t, N_HEADS * HEAD_PAD), lambda i: (i, 0)),
            pl.BlockSpec((t, N_HEADS * HEAD_PAD), lambda i: (i, 0)),
            pl.BlockSpec((N_HEADS * V_DIM, t), lambda i: (0, i)),
            pl.BlockSpec((t, dp), lambda i: (i, 0)),
        ],
        out_shape=[
            jax.ShapeDtypeStruct((rows, N_HEADS * HEAD_PAD), BF16),
            jax.ShapeDtypeStruct((rows, N_HEADS * HEAD_PAD), BF16),
            jax.ShapeDtypeStruct((N_HEADS * V_DIM, rows), BF16),
            jax.ShapeDtypeStruct((rows, dp), F32),
        ],
        compiler_params=_params(("arbitrary",)),
        name="in_proj",
    )(x, g, win, qn, wq, kvn, wk, wvt, cos_t, sin_t)


def _attn_kernel(q_ref, k_ref, vt_ref, o_ref, *, s_len, tk, n_sub, first_valid):
    q = q_ref[...]
    tq = q.shape[0]

    def update(s, vtb, carry):
        m, l, acc = carry
        m_new = jnp.maximum(m, jnp.max(s, axis=0, keepdims=True))
        p = jnp.exp2(s - m_new)
        alpha = jnp.exp2(m - m_new)
        l = alpha * l + jnp.sum(p, axis=0, keepdims=True)
        acc = alpha * acc + _dot(vtb, p.astype(BF16))
        return m_new, l, acc

    carry = (jnp.full((1, tq), NEG, F32), jnp.zeros((1, tq), F32), jnp.zeros((V_DIM, tq), F32))
    lo = ALIGN - LANES
    row = lax.broadcasted_iota(jnp.int32, (LANES, tq), 0) + lo
    s_lead = jnp.where(row >= first_valid, _dot_nt(k_ref[lo:ALIGN, :], q), NEG)
    carry = update(s_lead, vt_ref[:, lo:ALIGN], carry)

    def body(c, carry):
        offs = [pl.multiple_of(ALIGN + (c * n_sub + j) * tk, LANES) for j in range(n_sub)]
        s = _dot_nt(k_ref[pl.ds(offs[0], tk), :], q)
        for j, off in enumerate(offs):
            s_next = _dot_nt(k_ref[pl.ds(offs[j + 1], tk), :], q) if j + 1 < n_sub else None
            carry = update(s, vt_ref[:, pl.ds(off, tk)], carry)
            s = s_next
        return carry

    m, l, acc = lax.fori_loop(0, s_len // (tk * n_sub), body, carry)
    o_ref[...] = (acc / l).T.astype(BF16)


def _attention(q, k, vt, n_seq, s_len, first_valid):
    region = ALIGN + s_len
    units = region // ALIGN
    q_units = units if units <= MAX_Q_UNITS_WHOLE else max(u for u in range(1, MAX_Q_UNITS + 1) if units % u == 0)
    tq = q_units * ALIGN
    nq = units // q_units
    tk = _largest_tile(s_len, (512, 256))
    n_sub = _largest_tile(s_len // tk, (4, 2, 1) if q_units <= MAX_Q_UNITS else (2, 1))
    kern = functools.partial(_attn_kernel, s_len=s_len, tk=tk, n_sub=n_sub, first_valid=first_valid)
    return pl.pallas_call(
        kern,
        grid=(n_seq, N_HEADS, nq),
        in_specs=[
            pl.BlockSpec((tq, HEAD_PAD), lambda s, h, i: (s * nq + i, h)),
            pl.BlockSpec((region, HEAD_PAD), lambda s, h, i: (s, h)),
            pl.BlockSpec((V_DIM, region), lambda s, h, i: (h, s)),
        ],
        out_specs=pl.BlockSpec((tq, V_DIM), lambda s, h, i: (s * nq + i, h)),
        out_shape=jax.ShapeDtypeStruct((n_seq * region, N_HEADS * V_DIM), BF16),
        compiler_params=_params(("arbitrary", "arbitrary", "arbitrary")),
        name="attention",
    )(q, k, vt)


def _out_kernel(attn_ref, u_ref, uprev_ref, unext_ref, x_ref, pw_ref, ps_ref, woa_ref, wop_ref, o_ref,
                *, tiles_per_seq, first_valid, n_pos):
    t, dp = u_ref.shape
    pg = dp // len(POOL_WINDOWS)
    p0 = (pl.program_id(0) % tiles_per_seq) * t - first_valid
    u = u_ref[...]
    ucat = jnp.concatenate([uprev_ref[...], u, unext_ref[...]], axis=0).astype(BF16)
    tt = lax.broadcasted_iota(jnp.int32, (t, t + 2 * HALO), 0)
    jj = lax.broadcasted_iota(jnp.int32, (t, t + 2 * HALO), 1)
    rel = jj - HALO - tt
    pj = p0 - HALO + jj
    col_ok = (pj >= 0) & (pj < n_pos)
    pos = p0 + lax.broadcasted_iota(jnp.int32, (t, 1), 0)
    ps = ps_ref[...]
    pooled = []
    for gi, w in enumerate(POOL_WINDOWS):
        band = jnp.where((rel >= -(w // 2)) & (rel <= w // 2 - 1) & col_ok, 1.0, 0.0).astype(BF16)
        wsum = _dot(band, ucat[:, gi * pg:(gi + 1) * pg])
        cnt = jnp.minimum(pos + w // 2, n_pos) - jnp.maximum(pos - w // 2, 0)
        cnt = jnp.maximum(cnt, 1).astype(F32)
        centred = (wsum / cnt - u[:, gi * pg:(gi + 1) * pg]).astype(BF16)
        pooled.append(_dot(centred, pw_ref[gi]) * ps[:, gi * pg:(gi + 1) * pg])
    pool = jnp.concatenate(pooled, axis=1).astype(BF16)
    y = _dot(attn_ref[...], woa_ref[...]) + _dot(pool, wop_ref[...])
    o_ref[...] = x_ref[...] + y


def _out_proj(x, attn, u, tile_off, n_tiles, tiles_per_seq, pw, ps, woa, wop, dims, first_valid, n_pos):
    d, _, _, dp = dims
    t = ALIGN
    pg = dp // len(POOL_WINDOWS)
    da = N_HEADS * V_DIM
    hb = t // HALO
    last = n_tiles * hb - 1
    const = lambda i: (0, 0)
    kern = functools.partial(_out_kernel, tiles_per_seq=tiles_per_seq, first_valid=first_valid, n_pos=n_pos)
    return pl.pallas_call(
        kern,
        grid=(n_tiles,),
        in_specs=[
            pl.BlockSpec((t, da), lambda i: (i, 0)),
            pl.BlockSpec((t, dp), lambda i: (i, 0)),
            pl.BlockSpec((HALO, dp), lambda i: (jnp.maximum(i * hb - 1, 0), 0)),
            pl.BlockSpec((HALO, dp), lambda i: (jnp.minimum((i + 1) * hb, last), 0)),
            pl.BlockSpec((t, d), lambda i: (i + tile_off, 0)),
            pl.BlockSpec((len(POOL_WINDOWS), pg, pg), lambda i: (0, 0, 0)),
            pl.BlockSpec((1, dp), const),
            pl.BlockSpec((da, d), const),
            pl.BlockSpec((dp, d), const),
        ],
        out_specs=pl.BlockSpec((t, d), lambda i: (i + tile_off, 0)),
        out_shape=jax.ShapeDtypeStruct(x.shape, F32),
        input_output_aliases={4: 0},
        compiler_params=_params(("arbitrary",)),
        name="out_proj",
    )(attn, u, u, u, x, pw, ps, woa, wop)


def _swiglu(h, wg, wu, wd):
    a = _dot(h, wg)
    b = _dot(h, wu)
    act = a * (1.0 / (1.0 + jnp.exp(-a))) * b
    return _dot(act.astype(BF16), wd)


def _ffn_kernel(x_ref, g_ref, wg_ref, wu_ref, wd_ref, o_ref, h_ref):
    @pl.when(pl.program_id(1) == 0)
    def _():
        x = x_ref[...]
        h_ref[...] = _rms(x, g_ref[...]).astype(BF16)
        o_ref[...] = x

    o_ref[...] += _swiglu(h_ref[...], wg_ref[...], wu_ref[...], wd_ref[...])


def _dense_ffn(x, g, wg, wu, wd):
    n, d = x.shape
    dff = wg.shape[1]
    t = _largest_tile(n, (768, 512, 256))
    tf = _largest_tile(dff, (512, 256, 128))
    return pl.pallas_call(
        _ffn_kernel,
        grid=(n // t, dff // tf),
        in_specs=[
            pl.BlockSpec((t, d), lambda i, j: (i, 0)),
            pl.BlockSpec((1, d), lambda i, j: (0, 0)),
            pl.BlockSpec((d, tf), lambda i, j: (0, j)),
            pl.BlockSpec((d, tf), lambda i, j: (0, j)),
            pl.BlockSpec((tf, d), lambda i, j: (j, 0)),
        ],
        out_specs=pl.BlockSpec((t, d), lambda i, j: (i, 0)),
        out_shape=jax.ShapeDtypeStruct((n, d), F32),
        scratch_shapes=[pltpu.VMEM((t, d), BF16)],
        compiler_params=_params(("arbitrary", "arbitrary")),
        name="dense_ffn",
    )(x, g, wg, wu, wd)


def _router_kernel(x_ref, g_ref, wr_ref, e_ref, p_ref, *, n_exp):
    h = _rms(x_ref[...], g_ref[...])
    logits = jnp.dot(h, wr_ref[...], preferred_element_type=F32, precision=lax.Precision.HIGHEST)
    lane = lax.broadcasted_iota(jnp.int32, logits.shape, 1)
    logits = jnp.where(lane < n_exp, logits, -jnp.inf)
    lane_f = lane.astype(F32)
    m1 = jnp.max(logits, axis=1, keepdims=True)
    i1 = jnp.min(jnp.where(logits == m1, lane_f, float(LANES)), axis=1, keepdims=True)
    rest = jnp.where(lane_f == i1, -jnp.inf, logits)
    m2 = jnp.max(rest, axis=1, keepdims=True)
    i2 = jnp.min(jnp.where(rest == m2, lane_f, float(LANES)), axis=1, keepdims=True)
    e2 = jnp.exp(m2 - m1)
    den = 1.0 + e2
    e_ref[...] = jnp.where(lane == 0, i1, jnp.where(lane == 1, i2, 0.0)).astype(jnp.int32)
    p_ref[...] = jnp.where(lane == 0, 1.0 / den, jnp.where(lane == 1, e2 / den, 0.0))


def _router(x, g, wr, n_exp):
    n, d = x.shape
    t = _largest_tile(n, (768, 512, 256))
    return pl.pallas_call(
        functools.partial(_router_kernel, n_exp=n_exp),
        grid=(n // t,),
        in_specs=[
            pl.BlockSpec((t, d), lambda i: (i, 0)),
            pl.BlockSpec((1, d), lambda i: (0, 0)),
            pl.BlockSpec((d, LANES), lambda i: (0, 0)),
        ],
        out_specs=[pl.BlockSpec((t, LANES), lambda i: (i, 0)), pl.BlockSpec((t, LANES), lambda i: (i, 0))],
        out_shape=[jax.ShapeDtypeStruct((n, LANES), jnp.int32), jax.ShapeDtypeStruct((n, LANES), F32)],
        compiler_params=_params(("arbitrary",)),
        name="router",
    )(x, g, wr)


DMA_UNROLL = 8


def _row_copies(idx_ref, base, src_hbm, dst_ref, sem, lo, n, wait):
    def one(r, _):
        row = lo + r
        cp = pltpu.make_async_copy(src_hbm.at[pl.ds(idx_ref[base + row], 1), :], dst_ref.at[pl.ds(row, 1), :], sem)
        if wait:
            cp.wait()
        else:
            cp.start()
        return 0

    lax.fori_loop(0, n, one, 0, unroll=DMA_UNROLL)


def _moe_kernel(src_ref, blk_e_ref, n_act_ref, x_hbm, g_ref, wg_ref, wu_ref, wd_ref, o_ref, xbuf, h_ref, sem):
    b = pl.program_id(0)
    j = pl.program_id(1)
    r = o_ref.shape[0]
    per_step = r // pl.num_programs(1)
    n_act = n_act_ref[0]
    slot = b % 2

    @pl.when((b == 0) & (j == 0) & (n_act > 0))
    def _():
        _row_copies(src_ref, 0, x_hbm, xbuf.at[0], sem.at[0], 0, r, wait=False)

    @pl.when(j == 0)
    def _():
        o_ref[...] = jnp.zeros_like(o_ref)

        @pl.when(b < n_act)
        def _():
            _row_copies(src_ref, b * r, x_hbm, xbuf.at[slot], sem.at[slot], 0, r, wait=True)
            h_ref[...] = _rms(xbuf[slot], g_ref[...]).astype(BF16)

    @pl.when(b + 1 < n_act)
    def _():
        _row_copies(src_ref, (b + 1) * r, x_hbm, xbuf.at[1 - slot], sem.at[1 - slot], j * per_step, per_step,
                    wait=False)

    @pl.when(b < n_act)
    def _():
        o_ref[...] += _swiglu(h_ref[...], wg_ref[0], wu_ref[0], wd_ref[0])


def _moe_experts(x, src_tok, g, wg, wu, wd, blk_e, n_act):
    d = x.shape[1]
    n_rows = src_tok.shape[0]
    dff = wg.shape[2]
    r = EXPERT_BLOCK
    tf = next(c for c in (896, 1024, 512, 256, 128) if dff % c == 0 and r % (dff // c) == 0)
    nj = dff // tf

    def wmap_in(b, j, src, blk_e, n_act):
        return (blk_e[b], 0, jnp.where(b < n_act[0], j, nj - 1))

    def wmap_dn(b, j, src, blk_e, n_act):
        return (blk_e[b], jnp.where(b < n_act[0], j, nj - 1), 0)

    return pl.pallas_call(
        _moe_kernel,
        grid_spec=pltpu.PrefetchScalarGridSpec(
            num_scalar_prefetch=3,
            grid=(n_rows // r, nj),
            in_specs=[
                pl.BlockSpec(memory_space=pl.ANY),
                pl.BlockSpec((1, d), lambda b, j, *_: (0, 0)),
                pl.BlockSpec((1, d, tf), wmap_in),
                pl.BlockSpec((1, d, tf), wmap_in),
                pl.BlockSpec((1, tf, d), wmap_dn),
            ],
            out_specs=pl.BlockSpec((r, d), lambda b, j, *_: (b, 0)),
            scratch_shapes=[pltpu.VMEM((2, r, d), F32), pltpu.VMEM((r, d), BF16), pltpu.SemaphoreType.DMA((2,))],
        ),
        out_shape=jax.ShapeDtypeStruct((n_rows, d), F32),
        compiler_params=_params(("arbitrary", "arbitrary")),
        name="moe_experts",
    )(src_tok, blk_e, n_act, x, g, wg, wu, wd)


def _combine_kernel(pos_ref, x_ref, p_ref, fn_ref, y_hbm, o_ref, y0_ref, y1_ref, sem, *, tile_of, final_norm):
    t = x_ref.shape[0]
    base = tile_of(pl.program_id(0)) * t * TOP_K

    def copies(r):
        c0 = pltpu.make_async_copy(y_hbm.at[pl.ds(pos_ref[base + TOP_K * r], 1), :], y0_ref.at[pl.ds(r, 1), :], sem)
        c1 = pltpu.make_async_copy(y_hbm.at[pl.ds(pos_ref[base + TOP_K * r + 1], 1), :], y1_ref.at[pl.ds(r, 1), :], sem)
        return c0, c1

    def start(r, _):
        for c in copies(r):
            c.start()
        return 0

    def wait(r, _):
        for c in copies(r):
            c.wait()
        return 0

    lax.fori_loop(0, t, start, 0, unroll=DMA_UNROLL)
    lax.fori_loop(0, t, wait, 0, unroll=DMA_UNROLL)
    p = p_ref[...]
    y = x_ref[...] + (y0_ref[...] * p[:, 0:1] + y1_ref[...] * p[:, 1:2])
    if final_norm:
        y = _rms(y, fn_ref[...])
    o_ref[...] = y


def _combine(x, gates, pos, yb, fn, n_out_tiles, tile_of, final_norm):
    d = x.shape[1]
    t = ALIGN
    kern = functools.partial(_combine_kernel, tile_of=tile_of, final_norm=final_norm)
    return pl.pallas_call(
        kern,
        grid_spec=pltpu.PrefetchScalarGridSpec(
            num_scalar_prefetch=1,
            grid=(n_out_tiles,),
            in_specs=[
                pl.BlockSpec((t, d), lambda i, pos: (tile_of(i), 0)),
                pl.BlockSpec((t, LANES), lambda i, pos: (tile_of(i), 0)),
                pl.BlockSpec((1, d), lambda i, pos: (0, 0)),
                pl.BlockSpec(memory_space=pl.ANY),
            ],
            out_specs=pl.BlockSpec((t, d), lambda i, pos: (i, 0)),
            scratch_shapes=[pltpu.VMEM((t, d), F32), pltpu.VMEM((t, d), F32), pltpu.SemaphoreType.DMA],
        ),
        out_shape=jax.ShapeDtypeStruct((n_out_tiles * t, d), F32),
        compiler_params=_params(("arbitrary",)),
        name="moe_combine",
    )(pos, x, gates, fn, yb)


def _norm_kernel(x_ref, g_ref, o_ref):
    o_ref[...] = _rms(x_ref[...], g_ref[...])


def _final_norm(x, g, n_out_tiles, tile_of):
    d = x.shape[1]
    t = ALIGN
    return pl.pallas_call(
        _norm_kernel,
        grid=(n_out_tiles,),
        in_specs=[pl.BlockSpec((t, d), lambda i: (tile_of(i), 0)), pl.BlockSpec((1, d), lambda i: (0, 0))],
        out_specs=pl.BlockSpec((t, d), lambda i: (i, 0)),
        out_shape=jax.ShapeDtypeStruct((n_out_tiles * t, d), F32),
        compiler_params=_params(("arbitrary",)),
        name="final_norm",
    )(x, g)


def _routing_tables(top_e, valid, n_exp):
    n = top_e.shape[0]
    a = n * TOP_K
    e_flat = jnp.where(valid[:, None], top_e, n_exp).reshape(-1)
    onehot = (e_flat[:, None] == jnp.arange(n_exp, dtype=jnp.int32)[None, :]).astype(jnp.int32)
    csum = jnp.cumsum(onehot, axis=0)
    counts = csum[-1]
    rank = jnp.sum((csum - onehot) * onehot, axis=1)
    padded = (counts + EXPERT_BLOCK - 1) // EXPERT_BLOCK * EXPERT_BLOCK
    pad_end = jnp.cumsum(padded)
    pad_start = pad_end - padded
    n_blocks = -(-a // EXPERT_BLOCK) + n_exp
    n_rows = n_blocks * EXPERT_BLOCK
    e_safe = jnp.minimum(e_flat, n_exp - 1)
    routed = e_flat < n_exp
    dest = jnp.where(routed, pad_start[e_safe] + rank, n_rows)
    tok = jnp.arange(a, dtype=jnp.int32) // TOP_K
    src_tok = jnp.zeros((n_rows,), jnp.int32).at[dest].set(tok, mode="drop")
    blk_start = jnp.arange(n_blocks, dtype=jnp.int32) * EXPERT_BLOCK
    blk_e = jnp.minimum(jnp.sum(pad_end[None, :] <= blk_start[:, None], axis=1), n_exp - 1).astype(jnp.int32)
    n_act = (pad_end[-1] // EXPERT_BLOCK).astype(jnp.int32).reshape(1)
    last_e = blk_e[jnp.maximum(n_act[0] - 1, 0)]
    blk_e = jnp.where(jnp.arange(n_blocks) < n_act[0], blk_e, last_e)
    pos = jnp.where(routed, dest, 0).astype(jnp.int32)
    return src_tok, blk_e, n_act, pos


def _rope_tables(n_rows, first_valid):
    inv = ROPE_THETA ** (-jnp.arange(0, ROPE_DIM, 2, dtype=F32) / ROPE_DIM)
    pos = (jnp.arange(n_rows, dtype=jnp.int32) - first_valid).astype(F32)
    ang = pos[:, None] * inv[None, :]
    cos, sin = jnp.cos(ang), jnp.sin(ang)
    zeros = jnp.zeros((n_rows, LANES - ROPE_DIM), F32)
    return jnp.concatenate([cos, cos, zeros], axis=1), jnp.concatenate([-sin, sin, zeros], axis=1)


def kernel(x_prompt, x_sample, meta_tokens, mix_norm, w_in, q_norm, w_uq, kv_norm, w_ukv, pool_w, pool_scale, w_out, ffn_norm, dense_w_gate, dense_w_up, dense_w_down, moe_w_router, moe_w_gate, moe_w_up, moe_w_down, final_norm):
    depth, d = mix_norm.shape
    n_meta = meta_tokens.shape[0]
    ql = q_norm.shape[1]
    kvl = kv_norm.shape[1]
    dp = pool_scale.shape[1]
    n_exp = moe_w_router.shape[2]
    da = N_HEADS * V_DIM
    assert d == da + dp and dp % (len(POOL_WINDOWS) * LANES) == 0
    assert ql % LANES == 0 and kvl % LANES == 0 and n_meta <= LANES and n_meta % 8 == 0
    assert w_in.shape[2] == ql + kvl + ROPE_DIM + dp and n_exp <= LANES
    dims = (d, ql, kvl, dp)
    first_valid = ALIGN - n_meta

    groups = [(x_prompt.shape[0], x_prompt.shape[1]), (x_sample.shape[0], x_sample.shape[1])]
    for _, s_len in groups:
        assert s_len % ALIGN == 0
    lead = jnp.concatenate([jnp.zeros((first_valid, d), F32), meta_tokens.astype(F32)], axis=0)
    parts = []
    for xs in (x_prompt, x_sample):
        for bi in range(xs.shape[0]):
            parts += [lead, xs[bi]]
    x = jnp.concatenate(parts, axis=0)
    n_tok = x.shape[0]
    valid = jnp.concatenate(
        [jnp.tile(jnp.arange(ALIGN + s_len) >= first_valid, nb) for nb, s_len in groups])

    tables = [_rope_tables(ALIGN + s_len, first_valid) for _, s_len in groups]

    for l in range(depth):
        wi = w_in[l]
        c0, c1, c2 = ql, ql + kvl, ql + kvl + ROPE_DIM
        win = jnp.concatenate(
            [wi[:, :c1], wi[:, c2:], wi[:, c1:c2], jnp.zeros((d, LANES - ROPE_DIM), F32)], axis=1).astype(BF16)
        wq3 = w_uq[l].reshape(ql, N_HEADS, QK_DIM)
        wq = jnp.concatenate([wq3, jnp.zeros((ql, N_HEADS, HEAD_PAD - QK_DIM), F32)], axis=2)
        wq = wq.reshape(ql, N_HEADS * HEAD_PAD).astype(BF16)
        wkv3 = w_ukv[l].reshape(kvl, N_HEADS, NOPE_DIM + V_DIM)
        wk = wkv3[:, :, :NOPE_DIM].reshape(kvl, N_HEADS * NOPE_DIM).astype(BF16)
        wvt = wkv3[:, :, NOPE_DIM:].reshape(kvl, N_HEADS * V_DIM).T.astype(BF16)
        woa = w_out[l][:da].astype(BF16)
        wop = w_out[l][da:].astype(BF16)
        pw = pool_w[l].astype(BF16)

        tile_off = 0
        for (nb, s_len), (cos_t, sin_t) in zip(groups, tables):
            tps = (ALIGN + s_len) // ALIGN
            n_tiles = nb * tps
            q, k, vt, u = _in_proj(x, tile_off, n_tiles, tps, mix_norm[l][None], win, q_norm[l][None], wq,
                                   kv_norm[l][None], wk, wvt, cos_t, sin_t, dims)
            attn = _attention(q, k, vt, nb, s_len, first_valid)
            x = _out_proj(x, attn, u, tile_off, n_tiles, tps, pw, pool_scale[l][None], woa, wop, dims,
                          first_valid, n_meta + s_len)
            tile_off += n_tiles

        last = l == depth - 1
        if l % 2 == 0:
            i = l // 2
            x = _dense_ffn(x, ffn_norm[l][None], dense_w_gate[i].astype(BF16), dense_w_up[i].astype(BF16),
                           dense_w_down[i].astype(BF16))
            moe = None
        else:
            i = l // 2
            wr = jnp.concatenate([moe_w_router[i], jnp.zeros((d, LANES - n_exp), F32)], axis=1)
            top_e, gates = _router(x, ffn_norm[l][None], wr, n_exp)
            src_tok, blk_e, n_act, pos = _routing_tables(top_e[:, :TOP_K], valid, n_exp)
            yb = _moe_experts(x, src_tok, ffn_norm[l][None], moe_w_gate[i].astype(BF16), moe_w_up[i].astype(BF16),
                              moe_w_down[i].astype(BF16), blk_e, n_act)
            moe = (gates, pos, yb)
            if not last:
                x = _combine(x, gates, pos, yb, final_norm[None], n_tok // ALIGN, lambda i: i, False)
                moe = None

    outs = []
    tile_off = 0
    for (nb, s_len), xs in zip(groups, (x_prompt, x_sample)):
        real = s_len // ALIGN

        def tile_of(i, tile_off=tile_off, real=real):
            return tile_off + (i // real) * (real + 1) + 1 + i % real

        if moe is not None:
            gates, pos, yb = moe
            y = _combine(x, gates, pos, yb, final_norm[None], nb * real, tile_of, True)
        else:
            y = _final_norm(x, final_norm[None], nb * real, tile_of)
        outs.append(y.reshape(nb, s_len, d).astype(xs.dtype))
        tile_off += nb * (real + 1)
    return tuple(outs)
```

```python
import functools

import jax
import jax.numpy as jnp
from jax import lax
from jax.experimental import pallas as pl
from jax.experimental.pallas import tpu as pltpu

N_HEADS = 8
NOPE_DIM = 128
ROPE_DIM = 64
V_DIM = 128
QK_DIM = NOPE_DIM + ROPE_DIM
HEAD_PAD = 256
V_PAD = V_DIM + 16
POOL_WINDOWS = (2, 4, 8, 16)
HALO = 8
TOP_K = 2
EXPERT_BLOCK = 512
ROPE_THETA = 10000.0
RMS_EPS = 1e-6
LOG2E = 1.4426950408889634

LANES = 128
ALIGN = 256
MAX_Q_UNITS = 5
MAX_Q_UNITS_WHOLE = 17
NEG = -1e30
VMEM_LIMIT = 56 * 1024 * 1024

F32 = jnp.float32
BF16 = jnp.bfloat16


def _rms(x, g):
    ms = jnp.mean(x * x, axis=-1, keepdims=True)
    return x * lax.rsqrt(ms + RMS_EPS) * g


def _dot(a, b):
    return jnp.dot(a, b, preferred_element_type=F32)


def _dot_nt(a, b):
    return lax.dot_general(a, b, (((1,), (1,)), ((), ())), preferred_element_type=F32)


def _params(sem, flags=None):
    return pltpu.CompilerParams(dimension_semantics=sem, vmem_limit_bytes=VMEM_LIMIT, flags=flags)


def _largest_tile(n, candidates):
    for c in candidates:
        if n % c == 0:
            return c
    raise ValueError(f"no tile in {candidates} divides {n}")


def _in_kernel(x_ref, g_ref, win_ref, qn_ref, wq_ref, kvn_ref, wk_ref, wvt_ref, cos_ref, sin_ref,
               q_ref, k_ref, vt_ref, u_ref, *, ql, kvl, dp, scale):
    h = _rms(x_ref[...], g_ref[...]).astype(BF16)
    z = _dot(h, win_ref[...])
    cq = z[:, :ql]
    ckv = z[:, ql:ql + kvl]
    u_ref[...] = z[:, ql + kvl:ql + kvl + dp]
    kr = z[:, ql + kvl + dp:]
    cqn = _rms(cq, qn_ref[...]).astype(BF16)
    ckvn = _rms(ckv, kvn_ref[...]).astype(BF16)
    q = _dot(cqn, wq_ref[...])
    kn = _dot(ckvn, wk_ref[...])
    vt = _dot_nt(wvt_ref[...], ckvn).astype(BF16)
    ones = jnp.ones((V_PAD - V_DIM, vt.shape[1]), BF16)
    for hh in range(N_HEADS):
        vt_ref[hh * V_PAD:hh * V_PAD + V_DIM, :] = vt[hh * V_DIM:(hh + 1) * V_DIM]
        vt_ref[hh * V_PAD + V_DIM:(hh + 1) * V_PAD, :] = ones

    c = cos_ref[...]
    s = sin_ref[...]
    first_half = lax.broadcasted_iota(jnp.int32, c.shape, 1) < ROPE_DIM // 2

    def rope(v):
        partner = jnp.where(first_half, pltpu.roll(v, LANES - ROPE_DIM // 2, 1),
                            pltpu.roll(v, ROPE_DIM // 2, 1))
        return v * c + partner * s

    kr_r = rope(kr).astype(BF16)
    for hh in range(N_HEADS):
        lo = hh * HEAD_PAD
        q_ref[:, lo:lo + NOPE_DIM] = (q[:, lo:lo + NOPE_DIM] * scale).astype(BF16)
        q_ref[:, lo + NOPE_DIM:lo + HEAD_PAD] = (rope(q[:, lo + NOPE_DIM:lo + HEAD_PAD]) * scale).astype(BF16)
        k_ref[:, lo:lo + NOPE_DIM] = kn[:, hh * NOPE_DIM:(hh + 1) * NOPE_DIM].astype(BF16)
        k_ref[:, lo + NOPE_DIM:lo + HEAD_PAD] = kr_r


def _in_proj(x, tile_off, n_tiles, tiles_per_seq, g, win, qn, wq, kvn, wk, wvt, cos_t, sin_t, dims):
    d, ql, kvl, dp = dims
    t = ALIGN
    rows = n_tiles * t
    zw = win.shape[1]
    const = lambda i: (0, 0)
    kern = functools.partial(_in_kernel, ql=ql, kvl=kvl, dp=dp, scale=QK_DIM ** -0.5 * LOG2E)
    return pl.pallas_call(
        kern,
        grid=(n_tiles,),
        in_specs=[
            pl.BlockSpec((t, d), lambda i: (i + tile_off, 0)),
            pl.BlockSpec((1, d), const),
            pl.BlockSpec((d, zw), const),
            pl.BlockSpec((1, ql), const),
            pl.BlockSpec((ql, N_HEADS * HEAD_PAD), const),
            pl.BlockSpec((1, kvl), const),
            pl.BlockSpec((kvl, N_HEADS * NOPE_DIM), const),
            pl.BlockSpec((N_HEADS * V_DIM, kvl), const),
            pl.BlockSpec((t, LANES), lambda i: (i % tiles_per_seq, 0)),
            pl.BlockSpec((t, LANES), lambda i: (i % tiles_per_seq, 0)),
        ],
        out_specs=[
            pl.BlockSpec((t, N_HEADS * HEAD_PAD), lambda i: (i, 0)),
            pl.BlockSpec((t, N_HEADS * HEAD_PAD), lambda i: (i, 0)),
            pl.BlockSpec((N_HEADS * V_PAD, t), lambda i: (0, i)),
            pl.BlockSpec((t, dp), lambda i: (i, 0)),
        ],
        out_shape=[
            jax.ShapeDtypeStruct((rows, N_HEADS * HEAD_PAD), BF16),
            jax.ShapeDtypeStruct((rows, N_HEADS * HEAD_PAD), BF16),
            jax.ShapeDtypeStruct((N_HEADS * V_PAD, rows), BF16),
            jax.ShapeDtypeStruct((rows, dp), F32),
        ],
        compiler_params=_params(("arbitrary",)),
        name="in_proj",
    )(x, g, win, qn, wq, kvn, wk, wvt, cos_t, sin_t)


def _attn_kernel(q_ref, k_ref, vt_ref, o_ref, *, s_len, tk, n_sub, first_valid):
    q = q_ref[...]
    tq = q.shape[0]

    def update(s, vtb, carry):
        m, acc = carry
        m_new = jnp.maximum(m, jnp.max(s, axis=0, keepdims=True))
        p = jnp.exp2(s - m_new).astype(BF16)
        acc = jnp.exp2(m - m_new) * acc + _dot(vtb, p)
        return m_new, acc

    carry = (jnp.full((1, tq), NEG, F32), jnp.zeros((V_PAD, tq), F32))
    lo = ALIGN - LANES
    row = lax.broadcasted_iota(jnp.int32, (LANES, tq), 0) + lo
    s_lead = jnp.where(row >= first_valid, _dot_nt(k_ref[lo:ALIGN, :], q), NEG)
    carry = update(s_lead, vt_ref[:, lo:ALIGN], carry)

    def body(c, carry):
        offs = [pl.multiple_of(ALIGN + (c * n_sub + j) * tk, LANES) for j in range(n_sub)]
        s = _dot_nt(k_ref[pl.ds(offs[0], tk), :], q)
        for j, off in enumerate(offs):
            s_next = _dot_nt(k_ref[pl.ds(offs[j + 1], tk), :], q) if j + 1 < n_sub else None
            carry = update(s, vt_ref[:, pl.ds(off, tk)], carry)
            s = s_next
        return carry

    _, acc = lax.fori_loop(0, s_len // (tk * n_sub), body, carry)
    o_ref[...] = (acc[:V_DIM] / acc[V_DIM:V_DIM + 1]).T.astype(BF16)


def _attention(q, k, vt, n_seq, s_len, first_valid):
    region = ALIGN + s_len
    units = region // ALIGN
    q_units = units if units <= MAX_Q_UNITS_WHOLE else max(u for u in range(1, MAX_Q_UNITS + 1) if units % u == 0)
    tq = q_units * ALIGN
    nq = units // q_units
    tk = _largest_tile(s_len, (512, 256))
    n_sub = _largest_tile(s_len // tk, (8, 4, 2, 1) if q_units <= MAX_Q_UNITS else (2, 1))
    kern = functools.partial(_attn_kernel, s_len=s_len, tk=tk, n_sub=n_sub, first_valid=first_valid)
    return pl.pallas_call(
        kern,
        grid=(n_seq, N_HEADS, nq),
        in_specs=[
            pl.BlockSpec((tq, HEAD_PAD), lambda s, h, i: (s * nq + i, h)),
            pl.BlockSpec((region, HEAD_PAD), lambda s, h, i: (s, h)),
            pl.BlockSpec((V_PAD, region), lambda s, h, i: (h, s)),
        ],
        out_specs=pl.BlockSpec((tq, V_DIM), lambda s, h, i: (s * nq + i, h)),
        out_shape=jax.ShapeDtypeStruct((n_seq * region, N_HEADS * V_DIM), BF16),
        compiler_params=_params(("arbitrary", "arbitrary", "arbitrary")),
        name="attention",
    )(q, k, vt)


def _out_kernel(attn_ref, u_ref, uprev_ref, unext_ref, x_ref, pw_ref, ps_ref, woa_ref, wop_ref, o_ref,
                *, tiles_per_seq, first_valid, n_pos):
    t, dp = u_ref.shape
    pg = dp // len(POOL_WINDOWS)
    p0 = (pl.program_id(0) % tiles_per_seq) * t - first_valid
    u = u_ref[...]
    ucat = jnp.concatenate([uprev_ref[...], u, unext_ref[...]], axis=0).astype(BF16)
    tt = lax.broadcasted_iota(jnp.int32, (t, t + 2 * HALO), 0)
    jj = lax.broadcasted_iota(jnp.int32, (t, t + 2 * HALO), 1)
    rel = jj - HALO - tt
    pj = p0 - HALO + jj
    col_ok = (pj >= 0) & (pj < n_pos)
    pos = p0 + lax.broadcasted_iota(jnp.int32, (t, 1), 0)
    ps = ps_ref[...]
    pooled = []
    for gi, w in enumerate(POOL_WINDOWS):
        band = jnp.where((rel >= -(w // 2)) & (rel <= w // 2 - 1) & col_ok, 1.0, 0.0).astype(BF16)
        wsum = _dot(band, ucat[:, gi * pg:(gi + 1) * pg])
        cnt = jnp.minimum(pos + w // 2, n_pos) - jnp.maximum(pos - w // 2, 0)
        cnt = jnp.maximum(cnt, 1).astype(F32)
        centred = (wsum / cnt - u[:, gi * pg:(gi + 1) * pg]).astype(BF16)
        pooled.append(_dot(centred, pw_ref[gi]) * ps[:, gi * pg:(gi + 1) * pg])
    pool = jnp.concatenate(pooled, axis=1).astype(BF16)
    y = _dot(attn_ref[...], woa_ref[...]) + _dot(pool, wop_ref[...])
    o_ref[...] = x_ref[...] + y


def _out_proj(x, attn, u, tile_off, n_tiles, tiles_per_seq, pw, ps, woa, wop, dims, first_valid, n_pos):
    d, _, _, dp = dims
    t = ALIGN
    pg = dp // len(POOL_WINDOWS)
    da = N_HEADS * V_DIM
    hb = t // HALO
    last = n_tiles * hb - 1
    const = lambda i: (0, 0)
    kern = functools.partial(_out_kernel, tiles_per_seq=tiles_per_seq, first_valid=first_valid, n_pos=n_pos)
    return pl.pallas_call(
        kern,
        grid=(n_tiles,),
        in_specs=[
            pl.BlockSpec((t, da), lambda i: (i, 0)),
            pl.BlockSpec((t, dp), lambda i: (i, 0)),
            pl.BlockSpec((HALO, dp), lambda i: (jnp.maximum(i * hb - 1, 0), 0)),
            pl.BlockSpec((HALO, dp), lambda i: (jnp.minimum((i + 1) * hb, last), 0)),
            pl.BlockSpec((t, d), lambda i: (i + tile_off, 0)),
            pl.BlockSpec((len(POOL_WINDOWS), pg, pg), lambda i: (0, 0, 0)),
            pl.BlockSpec((1, dp), const),
            pl.BlockSpec((da, d), const),
            pl.BlockSpec((dp, d), const),
        ],
        out_specs=pl.BlockSpec((t, d), lambda i: (i + tile_off, 0)),
        out_shape=jax.ShapeDtypeStruct(x.shape, F32),
        input_output_aliases={4: 0},
        compiler_params=_params(("arbitrary",)),
        name="out_proj",
    )(attn, u, u, u, x, pw, ps, woa, wop)


def _swiglu(h, wg, wu, wd):
    a = _dot(h, wg)
    b = _dot(h, wu)
    act = a * (1.0 / (1.0 + jnp.exp(-a))) * b
    return _dot(act.astype(BF16), wd)


def _ffn_kernel(x_ref, g_ref, wg_ref, wu_ref, wd_ref, o_ref, h_ref):
    @pl.when(pl.program_id(1) == 0)
    def _():
        x = x_ref[...]
        h_ref[...] = _rms(x, g_ref[...]).astype(BF16)
        o_ref[...] = x

    o_ref[...] += _swiglu(h_ref[...], wg_ref[...], wu_ref[...], wd_ref[...])


def _dense_ffn(x, g, wg, wu, wd):
    n, d = x.shape
    dff = wg.shape[1]
    t = _largest_tile(n, (768, 512, 256))
    tf = _largest_tile(dff, (512, 256, 128))
    return pl.pallas_call(
        _ffn_kernel,
        grid=(n // t, dff // tf),
        in_specs=[
            pl.BlockSpec((t, d), lambda i, j: (i, 0)),
            pl.BlockSpec((1, d), lambda i, j: (0, 0)),
            pl.BlockSpec((d, tf), lambda i, j: (0, j)),
            pl.BlockSpec((d, tf), lambda i, j: (0, j)),
            pl.BlockSpec((tf, d), lambda i, j: (j, 0)),
        ],
        out_specs=pl.BlockSpec((t, d), lambda i, j: (i, 0)),
        out_shape=jax.ShapeDtypeStruct((n, d), F32),
        scratch_shapes=[pltpu.VMEM((t, d), BF16)],
        compiler_params=_params(("arbitrary", "arbitrary")),
        name="dense_ffn",
    )(x, g, wg, wu, wd)


def _router_kernel(x_ref, g_ref, wr_ref, e_ref, p_ref, *, n_exp):
    h = _rms(x_ref[...], g_ref[...])
    logits = jnp.dot(h, wr_ref[...], preferred_element_type=F32, precision=lax.Precision.HIGHEST)
    lane = lax.broadcasted_iota(jnp.int32, logits.shape, 1)
    logits = jnp.where(lane < n_exp, logits, -jnp.inf)
    lane_f = lane.astype(F32)
    m1 = jnp.max(logits, axis=1, keepdims=True)
    i1 = jnp.min(jnp.where(logits == m1, lane_f, float(LANES)), axis=1, keepdims=True)
    rest = jnp.where(lane_f == i1, -jnp.inf, logits)
    m2 = jnp.max(rest, axis=1, keepdims=True)
    i2 = jnp.min(jnp.where(rest == m2, lane_f, float(LANES)), axis=1, keepdims=True)
    e2 = jnp.exp(m2 - m1)
    den = 1.0 + e2
    e_ref[...] = jnp.where(lane == 0, i1, jnp.where(lane == 1, i2, 0.0)).astype(jnp.int32)
    p_ref[...] = jnp.where(lane == 0, 1.0 / den, jnp.where(lane == 1, e2 / den, 0.0))


def _router(x, g, wr, n_exp):
    n, d = x.shape
    t = _largest_tile(n, (768, 512, 256))
    return pl.pallas_call(
        functools.partial(_router_kernel, n_exp=n_exp),
        grid=(n // t,),
        in_specs=[
            pl.BlockSpec((t, d), lambda i: (i, 0)),
            pl.BlockSpec((1, d), lambda i: (0, 0)),
            pl.BlockSpec((d, LANES), lambda i: (0, 0)),
        ],
        out_specs=[pl.BlockSpec((t, LANES), lambda i: (i, 0)), pl.BlockSpec((t, LANES), lambda i: (i, 0))],
        out_shape=[jax.ShapeDtypeStruct((n, LANES), jnp.int32), jax.ShapeDtypeStruct((n, LANES), F32)],
        compiler_params=_params(("arbitrary",)),
        name="router",
    )(x, g, wr)


DMA_UNROLL = 8


def _row_copies(idx_ref, base, src_hbm, dst_ref, sem, lo, n, wait):
    def one(r, _):
        row = lo + r
        cp = pltpu.make_async_copy(src_hbm.at[pl.ds(idx_ref[base + row], 1), :], dst_ref.at[pl.ds(row, 1), :], sem)
        if wait:
            cp.wait()
        else:
            cp.start()
        return 0

    lax.fori_loop(0, n, one, 0, unroll=DMA_UNROLL)


def _moe_kernel(src_ref, blk_e_ref, n_act_ref, x_hbm, g_ref, wg_ref, wu_ref, wd_ref, o_ref, xbuf, h_ref, sem, *, nj):
    b = pl.program_id(0)
    j = pl.program_id(1)
    r = o_ref.shape[0]
    n_act = n_act_ref[0]
    slot = b % 2
    per_step = -(-r // (nj * DMA_UNROLL)) * DMA_UNROLL
    last_step = r - (nj - 1) * per_step
    assert last_step > 0

    @pl.when((b == 0) & (j == 0) & (n_act > 0))
    def _():
        _row_copies(src_ref, 0, x_hbm, xbuf.at[0], sem.at[0], 0, r, wait=False)

    @pl.when(j == 0)
    def _():
        o_ref[...] = jnp.zeros_like(o_ref)

        @pl.when(b < n_act)
        def _():
            _row_copies(src_ref, b * r, x_hbm, xbuf.at[slot], sem.at[slot], 0, r, wait=True)
            h_ref[...] = _rms(xbuf[slot], g_ref[...]).astype(BF16)

    def prefetch(n):
        _row_copies(src_ref, (b + 1) * r, x_hbm, xbuf.at[1 - slot], sem.at[1 - slot], j * per_step, n, wait=False)

    if nj > 1:
        pl.when((b + 1 < n_act) & (j < nj - 1))(lambda: prefetch(per_step))
    pl.when((b + 1 < n_act) & (j == nj - 1))(lambda: prefetch(last_step))

    @pl.when(b < n_act)
    def _():
        o_ref[...] += _swiglu(h_ref[...], wg_ref[0], wu_ref[0], wd_ref[0])


def _moe_experts(x, src_tok, g, wg, wu, wd, blk_e, n_act):
    d = x.shape[1]
    n_rows = src_tok.shape[0]
    dff = wg.shape[2]
    r = EXPERT_BLOCK
    tf = _largest_tile(dff, (1024, 512, 256, 128))
    nj = dff // tf

    def wmap_in(b, j, src, blk_e, n_act):
        return (blk_e[b], 0, jnp.where(b < n_act[0], j, nj - 1))

    def wmap_dn(b, j, src, blk_e, n_act):
        return (blk_e[b], jnp.where(b < n_act[0], j, nj - 1), 0)

    return pl.pallas_call(
        functools.partial(_moe_kernel, nj=nj),
        grid_spec=pltpu.PrefetchScalarGridSpec(
            num_scalar_prefetch=3,
            grid=(n_rows // r, nj),
            in_specs=[
                pl.BlockSpec(memory_space=pl.ANY),
                pl.BlockSpec((1, d), lambda b, j, *_: (0, 0)),
                pl.BlockSpec((1, d, tf), wmap_in),
                pl.BlockSpec((1, d, tf), wmap_in),
                pl.BlockSpec((1, tf, d), wmap_dn),
            ],
            out_specs=pl.BlockSpec((r, d), lambda b, j, *_: (b, 0)),
            scratch_shapes=[pltpu.VMEM((2, r, d), F32), pltpu.VMEM((r, d), BF16), pltpu.SemaphoreType.DMA((2,))],
        ),
        out_shape=jax.ShapeDtypeStruct((n_rows, d), F32),
        compiler_params=_params(("arbitrary", "arbitrary")),
        name="moe_experts",
    )(src_tok, blk_e, n_act, x, g, wg, wu, wd)


def _combine_kernel(pos_ref, x_ref, p_ref, fn_ref, y_hbm, o_ref, y0_ref, y1_ref, sem, *, tile_of, final_norm):
    t = x_ref.shape[0]
    base = tile_of(pl.program_id(0)) * t * TOP_K

    def copies(r):
        c0 = pltpu.make_async_copy(y_hbm.at[pl.ds(pos_ref[base + TOP_K * r], 1), :], y0_ref.at[pl.ds(r, 1), :], sem)
        c1 = pltpu.make_async_copy(y_hbm.at[pl.ds(pos_ref[base + TOP_K * r + 1], 1), :], y1_ref.at[pl.ds(r, 1), :], sem)
        return c0, c1

    def start(r, _):
        for c in copies(r):
            c.start()
        return 0

    def wait(r, _):
        for c in copies(r):
            c.wait()
        return 0

    lax.fori_loop(0, t, start, 0, unroll=DMA_UNROLL)
    lax.fori_loop(0, t, wait, 0, unroll=DMA_UNROLL)
    p = p_ref[...]
    y = x_ref[...] + (y0_ref[...] * p[:, 0:1] + y1_ref[...] * p[:, 1:2])
    if final_norm:
        y = _rms(y, fn_ref[...])
    o_ref[...] = y


def _combine(x, gates, pos, yb, fn, n_out_tiles, tile_of, final_norm):
    d = x.shape[1]
    t = ALIGN
    kern = functools.partial(_combine_kernel, tile_of=tile_of, final_norm=final_norm)
    return pl.pallas_call(
        kern,
        grid_spec=pltpu.PrefetchScalarGridSpec(
            num_scalar_prefetch=1,
            grid=(n_out_tiles,),
            in_specs=[
                pl.BlockSpec((t, d), lambda i, pos: (tile_of(i), 0)),
                pl.BlockSpec((t, LANES), lambda i, pos: (tile_of(i), 0)),
                pl.BlockSpec((1, d), lambda i, pos: (0, 0)),
                pl.BlockSpec(memory_space=pl.ANY),
            ],
            out_specs=pl.BlockSpec((t, d), lambda i, pos: (i, 0)),
            scratch_shapes=[pltpu.VMEM((t, d), F32), pltpu.VMEM((t, d), F32), pltpu.SemaphoreType.DMA],
        ),
        out_shape=jax.ShapeDtypeStruct((n_out_tiles * t, d), F32),
        compiler_params=_params(("arbitrary",)),
        name="moe_combine",
    )(pos, x, gates, fn, yb)


def _norm_kernel(x_ref, g_ref, o_ref):
    o_ref[...] = _rms(x_ref[...], g_ref[...])


def _final_norm(x, g, n_out_tiles, tile_of):
    d = x.shape[1]
    t = ALIGN
    return pl.pallas_call(
        _norm_kernel,
        grid=(n_out_tiles,),
        in_specs=[pl.BlockSpec((t, d), lambda i: (tile_of(i), 0)), pl.BlockSpec((1, d), lambda i: (0, 0))],
        out_specs=pl.BlockSpec((t, d), lambda i: (i, 0)),
        out_shape=jax.ShapeDtypeStruct((n_out_tiles * t, d), F32),
        compiler_params=_params(("arbitrary",)),
        name="final_norm",
    )(x, g)


def _routing_tables(top_e, valid, n_exp):
    n = top_e.shape[0]
    a = n * TOP_K
    e_flat = jnp.where(valid[:, None], top_e, n_exp).reshape(-1)
    onehot = (e_flat[:, None] == jnp.arange(n_exp, dtype=jnp.int32)[None, :]).astype(jnp.int32)
    csum = jnp.cumsum(onehot, axis=0)
    counts = csum[-1]
    rank = jnp.sum((csum - onehot) * onehot, axis=1)
    padded = (counts + EXPERT_BLOCK - 1) // EXPERT_BLOCK * EXPERT_BLOCK
    pad_end = jnp.cumsum(padded)
    pad_start = pad_end - padded
    n_blocks = -(-a // EXPERT_BLOCK) + n_exp
    n_rows = n_blocks * EXPERT_BLOCK
    e_safe = jnp.minimum(e_flat, n_exp - 1)
    routed = e_flat < n_exp
    dest = jnp.where(routed, pad_start[e_safe] + rank, n_rows)
    tok = jnp.arange(a, dtype=jnp.int32) // TOP_K
    src_tok = jnp.zeros((n_rows,), jnp.int32).at[dest].set(tok, mode="drop")
    blk_start = jnp.arange(n_blocks, dtype=jnp.int32) * EXPERT_BLOCK
    blk_e = jnp.minimum(jnp.sum(pad_end[None, :] <= blk_start[:, None], axis=1), n_exp - 1).astype(jnp.int32)
    n_act = (pad_end[-1] // EXPERT_BLOCK).astype(jnp.int32).reshape(1)
    last_e = blk_e[jnp.maximum(n_act[0] - 1, 0)]
    blk_e = jnp.where(jnp.arange(n_blocks) < n_act[0], blk_e, last_e)
    pos = jnp.where(routed, dest, 0).astype(jnp.int32)
    return src_tok, blk_e, n_act, pos


def _rope_tables(n_rows, first_valid):
    inv = ROPE_THETA ** (-jnp.arange(0, ROPE_DIM, 2, dtype=F32) / ROPE_DIM)
    pos = (jnp.arange(n_rows, dtype=jnp.int32) - first_valid).astype(F32)
    ang = pos[:, None] * inv[None, :]
    cos, sin = jnp.cos(ang), jnp.sin(ang)
    zeros = jnp.zeros((n_rows, LANES - ROPE_DIM), F32)
    return jnp.concatenate([cos, cos, zeros], axis=1), jnp.concatenate([-sin, sin, zeros], axis=1)


def kernel(x_prompt, x_sample, meta_tokens, mix_norm, w_in, q_norm, w_uq, kv_norm, w_ukv, pool_w, pool_scale, w_out, ffn_norm, dense_w_gate, dense_w_up, dense_w_down, moe_w_router, moe_w_gate, moe_w_up, moe_w_down, final_norm):
    depth, d = mix_norm.shape
    n_meta = meta_tokens.shape[0]
    ql = q_norm.shape[1]
    kvl = kv_norm.shape[1]
    dp = pool_scale.shape[1]
    n_exp = moe_w_router.shape[2]
    da = N_HEADS * V_DIM
    assert d == da + dp and dp % (len(POOL_WINDOWS) * LANES) == 0
    assert ql % LANES == 0 and kvl % LANES == 0 and n_meta <= LANES and n_meta % 8 == 0
    assert w_in.shape[2] == ql + kvl + ROPE_DIM + dp and n_exp <= LANES
    dims = (d, ql, kvl, dp)
    first_valid = ALIGN - n_meta

    groups = [(x_prompt.shape[0], x_prompt.shape[1]), (x_sample.shape[0], x_sample.shape[1])]
    for _, s_len in groups:
        assert s_len % ALIGN == 0
    lead = jnp.concatenate([jnp.zeros((first_valid, d), F32), meta_tokens.astype(F32)], axis=0)
    parts = []
    for xs in (x_prompt, x_sample):
        for bi in range(xs.shape[0]):
            parts += [lead, xs[bi]]
    x = jnp.concatenate(parts, axis=0)
    n_tok = x.shape[0]
    valid = jnp.concatenate(
        [jnp.tile(jnp.arange(ALIGN + s_len) >= first_valid, nb) for nb, s_len in groups])

    tables = [_rope_tables(ALIGN + s_len, first_valid) for _, s_len in groups]

    for l in range(depth):
        wi = w_in[l]
        c0, c1, c2 = ql, ql + kvl, ql + kvl + ROPE_DIM
        win = jnp.concatenate(
            [wi[:, :c1], wi[:, c2:], wi[:, c1:c2], jnp.zeros((d, LANES - ROPE_DIM), F32)], axis=1).astype(BF16)
        wq3 = w_uq[l].reshape(ql, N_HEADS, QK_DIM)
        wq = jnp.concatenate([wq3, jnp.zeros((ql, N_HEADS, HEAD_PAD - QK_DIM), F32)], axis=2)
        wq = wq.reshape(ql, N_HEADS * HEAD_PAD).astype(BF16)
        wkv3 = w_ukv[l].reshape(kvl, N_HEADS, NOPE_DIM + V_DIM)
        wk = wkv3[:, :, :NOPE_DIM].reshape(kvl, N_HEADS * NOPE_DIM).astype(BF16)
        wvt = wkv3[:, :, NOPE_DIM:].reshape(kvl, N_HEADS * V_DIM).T.astype(BF16)
        woa = w_out[l][:da].astype(BF16)
        wop = w_out[l][da:].astype(BF16)
        pw = pool_w[l].astype(BF16)

        tile_off = 0
        for (nb, s_len), (cos_t, sin_t) in zip(groups, tables):
            tps = (ALIGN + s_len) // ALIGN
            n_tiles = nb * tps
            q, k, vt, u = _in_proj(x, tile_off, n_tiles, tps, mix_norm[l][None], win, q_norm[l][None], wq,
                                   kv_norm[l][None], wk, wvt, cos_t, sin_t, dims)
            attn = _attention(q, k, vt, nb, s_len, first_valid)
            x = _out_proj(x, attn, u, tile_off, n_tiles, tps, pw, pool_scale[l][None], woa, wop, dims,
                          first_valid, n_meta + s_len)
            tile_off += n_tiles

        last = l == depth - 1
        if l % 2 == 0:
            i = l // 2
            x = _dense_ffn(x, ffn_norm[l][None], dense_w_gate[i].astype(BF16), dense_w_up[i].astype(BF16),
                           dense_w_down[i].astype(BF16))
            moe = None
        else:
            i = l // 2
            wr = jnp.concatenate([moe_w_router[i], jnp.zeros((d, LANES - n_exp), F32)], axis=1)
            top_e, gates = _router(x, ffn_norm[l][None], wr, n_exp)
            src_tok, blk_e, n_act, pos = _routing_tables(top_e[:, :TOP_K], valid, n_exp)
            yb = _moe_experts(x, src_tok, ffn_norm[l][None], moe_w_gate[i].astype(BF16), moe_w_up[i].astype(BF16),
                              moe_w_down[i].astype(BF16), blk_e, n_act)
            moe = (gates, pos, yb)
            if not last:
                x = _combine(x, gates, pos, yb, final_norm[None], n_tok // ALIGN, lambda i: i, False)
                moe = None

    outs = []
    tile_off = 0
    for (nb, s_len), xs in zip(groups, (x_prompt, x_sample)):
        real = s_len // ALIGN

        def tile_of(i, tile_off=tile_off, real=real):
            return tile_off + (i // real) * (real + 1) + 1 + i % real

        if moe is not None:
            gates, pos, yb = moe
            y = _combine(x, gates, pos, yb, final_norm[None], nb * real, tile_of, True)
        else:
            y = _final_norm(x, final_norm[None], nb * real, tile_of)
        outs.append(y.reshape(nb, s_len, d).astype(xs.dtype))
        tile_off += nb * (real + 1)
    return tuple(outs)
```

```python
import functools

import jax
import jax.numpy as jnp
from jax import lax
from jax.experimental import pallas as pl
from jax.experimental.pallas import tpu as pltpu

N_HEADS = 8
NOPE_DIM = 128
ROPE_DIM = 64
V_DIM = 128
QK_DIM = NOPE_DIM + ROPE_DIM
HEAD_PAD = 256
V_PAD = V_DIM + 16
POOL_WINDOWS = (2, 4, 8, 16)
HALO = 8
TOP_K = 2
EXPERT_BLOCK = 512
ROPE_THETA = 10000.0
RMS_EPS = 1e-6
LOG2E = 1.4426950408889634

LANES = 128
ALIGN = 256
LOOKAHEAD = 2
MAX_Q_UNITS = 5
MAX_Q_UNITS_WHOLE = 17
NEG = -1e30
VMEM_LIMIT = 56 * 1024 * 1024

F32 = jnp.float32
BF16 = jnp.bfloat16


def _rms(x, g):
    ms = jnp.mean(x * x, axis=-1, keepdims=True)
    return x * lax.rsqrt(ms + RMS_EPS) * g


def _dot(a, b):
    return jnp.dot(a, b, preferred_element_type=F32)


def _dot_nt(a, b):
    return lax.dot_general(a, b, (((1,), (1,)), ((), ())), preferred_element_type=F32)


def _params(sem, flags=None):
    return pltpu.CompilerParams(dimension_semantics=sem, vmem_limit_bytes=VMEM_LIMIT, flags=flags)


def _largest_tile(n, candidates):
    for c in candidates:
        if n % c == 0:
            return c
    raise ValueError(f"no tile in {candidates} divides {n}")


def _in_kernel(x_ref, g_ref, win_ref, qn_ref, wq_ref, kvn_ref, wk_ref, wvt_ref, cos_ref, sin_ref,
               q_ref, k_ref, vt_ref, u_ref, *, ql, kvl, dp, scale):
    h = _rms(x_ref[...], g_ref[...]).astype(BF16)
    z = _dot(h, win_ref[...])
    cq = z[:, :ql]
    ckv = z[:, ql:ql + kvl]
    u_ref[...] = z[:, ql + kvl:ql + kvl + dp]
    kr = z[:, ql + kvl + dp:]
    cqn = _rms(cq, qn_ref[...]).astype(BF16)
    ckvn = _rms(ckv, kvn_ref[...]).astype(BF16)
    q = _dot(cqn, wq_ref[...])
    kn = _dot(ckvn, wk_ref[...])
    vt = _dot_nt(wvt_ref[...], ckvn).astype(BF16)
    ones = jnp.ones((V_PAD - V_DIM, vt.shape[1]), BF16)
    for hh in range(N_HEADS):
        vt_ref[hh * V_PAD:hh * V_PAD + V_DIM, :] = vt[hh * V_DIM:(hh + 1) * V_DIM]
        vt_ref[hh * V_PAD + V_DIM:(hh + 1) * V_PAD, :] = ones

    c = cos_ref[...]
    s = sin_ref[...]
    first_half = lax.broadcasted_iota(jnp.int32, c.shape, 1) < ROPE_DIM // 2

    def rope(v):
        partner = jnp.where(first_half, pltpu.roll(v, LANES - ROPE_DIM // 2, 1),
                            pltpu.roll(v, ROPE_DIM // 2, 1))
        return v * c + partner * s

    kr_r = rope(kr).astype(BF16)
    for hh in range(N_HEADS):
        lo = hh * HEAD_PAD
        q_ref[:, lo:lo + NOPE_DIM] = (q[:, lo:lo + NOPE_DIM] * scale).astype(BF16)
        q_ref[:, lo + NOPE_DIM:lo + HEAD_PAD] = (rope(q[:, lo + NOPE_DIM:lo + HEAD_PAD]) * scale).astype(BF16)
        k_ref[:, lo:lo + NOPE_DIM] = kn[:, hh * NOPE_DIM:(hh + 1) * NOPE_DIM].astype(BF16)
        k_ref[:, lo + NOPE_DIM:lo + HEAD_PAD] = kr_r


def _in_proj(x, tile_off, n_tiles, tiles_per_seq, g, win, qn, wq, kvn, wk, wvt, cos_t, sin_t, dims):
    d, ql, kvl, dp = dims
    t = ALIGN
    rows = n_tiles * t
    zw = win.shape[1]
    const = lambda i: (0, 0)
    kern = functools.partial(_in_kernel, ql=ql, kvl=kvl, dp=dp, scale=QK_DIM ** -0.5 * LOG2E)
    return pl.pallas_call(
        kern,
        grid=(n_tiles,),
        in_specs=[
            pl.BlockSpec((t, d), lambda i: (i + tile_off, 0)),
            pl.BlockSpec((1, d), const),
            pl.BlockSpec((d, zw), const),
            pl.BlockSpec((1, ql), const),
            pl.BlockSpec((ql, N_HEADS * HEAD_PAD), const),
            pl.BlockSpec((1, kvl), const),
            pl.BlockSpec((kvl, N_HEADS * NOPE_DIM), const),
            pl.BlockSpec((N_HEADS * V_DIM, kvl), const),
            pl.BlockSpec((t, LANES), lambda i: (i % tiles_per_seq, 0)),
            pl.BlockSpec((t, LANES), lambda i: (i % tiles_per_seq, 0)),
        ],
        out_specs=[
            pl.BlockSpec((t, N_HEADS * HEAD_PAD), lambda i: (i, 0)),
            pl.BlockSpec((t, N_HEADS * HEAD_PAD), lambda i: (i, 0)),
            pl.BlockSpec((N_HEADS * V_PAD, t), lambda i: (0, i)),
            pl.BlockSpec((t, dp), lambda i: (i, 0)),
        ],
        out_shape=[
            jax.ShapeDtypeStruct((rows, N_HEADS * HEAD_PAD), BF16),
            jax.ShapeDtypeStruct((rows, N_HEADS * HEAD_PAD), BF16),
            jax.ShapeDtypeStruct((N_HEADS * V_PAD, rows), BF16),
            jax.ShapeDtypeStruct((rows, dp), F32),
        ],
        compiler_params=_params(("arbitrary",)),
        name="in_proj",
    )(x, g, win, qn, wq, kvn, wk, wvt, cos_t, sin_t)


def _attn_kernel(q_ref, k_ref, vt_ref, o_ref, *, s_len, tk, n_sub, first_valid):
    q = q_ref[...]
    tq = q.shape[0]

    def update(s, vtb, carry):
        m, acc = carry
        m_new = jnp.maximum(m, jnp.max(s, axis=0, keepdims=True))
        p = jnp.exp2(s - m_new).astype(BF16)
        acc = jnp.exp2(m - m_new) * acc + _dot(vtb, p)
        return m_new, acc

    def score(kb):
        return _dot_nt(kb, q)

    carry = (jnp.full((1, tq), NEG, F32), jnp.zeros((V_PAD, tq), F32))
    lo = ALIGN - LANES
    row = lax.broadcasted_iota(jnp.int32, (LANES, tq), 0) + lo
    s_lead = jnp.where(row >= first_valid, _dot_nt(k_ref[lo:ALIGN, :], q), NEG)
    carry = update(s_lead, vt_ref[:, lo:ALIGN], carry)

    def body(c, carry):
        offs = [pl.multiple_of(ALIGN + (c * n_sub + j) * tk, LANES) for j in range(n_sub)]
        scores = [score(k_ref[pl.ds(off, tk), :]) for off in offs[:LOOKAHEAD]]
        for j, off in enumerate(offs):
            if j + LOOKAHEAD < n_sub:
                scores.append(score(k_ref[pl.ds(offs[j + LOOKAHEAD], tk), :]))
            carry = update(scores[j], vt_ref[:, pl.ds(off, tk)], carry)
        return carry

    _, acc = lax.fori_loop(0, s_len // (tk * n_sub), body, carry)
    o_ref[...] = (acc[:V_DIM] / acc[V_DIM:V_DIM + 1]).T.astype(BF16)


def _attention(q, k, vt, n_seq, s_len, first_valid):
    region = ALIGN + s_len
    units = region // ALIGN
    q_units = units if units <= MAX_Q_UNITS_WHOLE else max(u for u in range(1, MAX_Q_UNITS + 1) if units % u == 0)
    tq = q_units * ALIGN
    nq = units // q_units
    tk = _largest_tile(s_len, (512, 256))
    n_sub = _largest_tile(s_len // tk, (8, 4, 2, 1) if q_units <= MAX_Q_UNITS else (2, 1))
    kern = functools.partial(_attn_kernel, s_len=s_len, tk=tk, n_sub=n_sub, first_valid=first_valid)
    return pl.pallas_call(
        kern,
        grid=(n_seq, N_HEADS, nq),
        in_specs=[
            pl.BlockSpec((tq, HEAD_PAD), lambda s, h, i: (s * nq + i, h)),
            pl.BlockSpec((region, HEAD_PAD), lambda s, h, i: (s, h)),
            pl.BlockSpec((V_PAD, region), lambda s, h, i: (h, s)),
        ],
        out_specs=pl.BlockSpec((tq, V_DIM), lambda s, h, i: (s * nq + i, h)),
        out_shape=jax.ShapeDtypeStruct((n_seq * region, N_HEADS * V_DIM), BF16),
        compiler_params=_params(("arbitrary", "arbitrary", "arbitrary")),
        name="attention",
    )(q, k, vt)


def _out_kernel(attn_ref, u_ref, uprev_ref, unext_ref, x_ref, pw_ref, ps_ref, woa_ref, wop_ref, o_ref,
                *, tiles_per_seq, first_valid, n_pos):
    t, dp = u_ref.shape
    pg = dp // len(POOL_WINDOWS)
    p0 = (pl.program_id(0) % tiles_per_seq) * t - first_valid
    u = u_ref[...]
    ucat = jnp.concatenate([uprev_ref[...], u, unext_ref[...]], axis=0).astype(BF16)
    tt = lax.broadcasted_iota(jnp.int32, (t, t + 2 * HALO), 0)
    jj = lax.broadcasted_iota(jnp.int32, (t, t + 2 * HALO), 1)
    rel = jj - HALO - tt
    pj = p0 - HALO + jj
    col_ok = (pj >= 0) & (pj < n_pos)
    pos = p0 + lax.broadcasted_iota(jnp.int32, (t, 1), 0)
    ps = ps_ref[...]
    pooled = []
    for gi, w in enumerate(POOL_WINDOWS):
        band = jnp.where((rel >= -(w // 2)) & (rel <= w // 2 - 1) & col_ok, 1.0, 0.0).astype(BF16)
        wsum = _dot(band, ucat[:, gi * pg:(gi + 1) * pg])
        cnt = jnp.minimum(pos + w // 2, n_pos) - jnp.maximum(pos - w // 2, 0)
        cnt = jnp.maximum(cnt, 1).astype(F32)
        centred = (wsum / cnt - u[:, gi * pg:(gi + 1) * pg]).astype(BF16)
        pooled.append(_dot(centred, pw_ref[gi]) * ps[:, gi * pg:(gi + 1) * pg])
    pool = jnp.concatenate(pooled, axis=1).astype(BF16)
    y = _dot(attn_ref[...], woa_ref[...]) + _dot(pool, wop_ref[...])
    o_ref[...] = x_ref[...] + y


def _out_proj(x, attn, u, tile_off, n_tiles, tiles_per_seq, pw, ps, woa, wop, dims, first_valid, n_pos):
    d, _, _, dp = dims
    t = ALIGN
    pg = dp // len(POOL_WINDOWS)
    da = N_HEADS * V_DIM
    hb = t // HALO
    last = n_tiles * hb - 1
    const = lambda i: (0, 0)
    kern = functools.partial(_out_kernel, tiles_per_seq=tiles_per_seq, first_valid=first_valid, n_pos=n_pos)
    return pl.pallas_call(
        kern,
        grid=(n_tiles,),
        in_specs=[
            pl.BlockSpec((t, da), lambda i: (i, 0)),
            pl.BlockSpec((t, dp), lambda i: (i, 0)),
            pl.BlockSpec((HALO, dp), lambda i: (jnp.maximum(i * hb - 1, 0), 0)),
            pl.BlockSpec((HALO, dp), lambda i: (jnp.minimum((i + 1) * hb, last), 0)),
            pl.BlockSpec((t, d), lambda i: (i + tile_off, 0)),
            pl.BlockSpec((len(POOL_WINDOWS), pg, pg), lambda i: (0, 0, 0)),
            pl.BlockSpec((1, dp), const),
            pl.BlockSpec((da, d), const),
            pl.BlockSpec((dp, d), const),
        ],
        out_specs=pl.BlockSpec((t, d), lambda i: (i + tile_off, 0)),
        out_shape=jax.ShapeDtypeStruct(x.shape, F32),
        input_output_aliases={4: 0},
        compiler_params=_params(("arbitrary",)),
        name="out_proj",
    )(attn, u, u, u, x, pw, ps, woa, wop)


def _swiglu(h, wg, wu, wd):
    a = _dot(h, wg)
    b = _dot(h, wu)
    act = a * (1.0 / (1.0 + jnp.exp(-a))) * b
    return _dot(act.astype(BF16), wd)


def _ffn_kernel(x_ref, g_ref, wg_ref, wu_ref, wd_ref, o_ref, h_ref):
    @pl.when(pl.program_id(1) == 0)
    def _():
        x = x_ref[...]
        h_ref[...] = _rms(x, g_ref[...]).astype(BF16)
        o_ref[...] = x

    o_ref[...] += _swiglu(h_ref[...], wg_ref[...], wu_ref[...], wd_ref[...])


def _dense_ffn(x, g, wg, wu, wd):
    n, d = x.shape
    dff = wg.shape[1]
    t = _largest_tile(n, (768, 512, 256))
    tf = _largest_tile(dff, (512, 256, 128))
    return pl.pallas_call(
        _ffn_kernel,
        grid=(n // t, dff // tf),
        in_specs=[
            pl.BlockSpec((t, d), lambda i, j: (i, 0)),
            pl.BlockSpec((1, d), lambda i, j: (0, 0)),
            pl.BlockSpec((d, tf), lambda i, j: (0, j)),
            pl.BlockSpec((d, tf), lambda i, j: (0, j)),
            pl.BlockSpec((tf, d), lambda i, j: (j, 0)),
        ],
        out_specs=pl.BlockSpec((t, d), lambda i, j: (i, 0)),
        out_shape=jax.ShapeDtypeStruct((n, d), F32),
        scratch_shapes=[pltpu.VMEM((t, d), BF16)],
        compiler_params=_params(("arbitrary", "arbitrary")),
        name="dense_ffn",
    )(x, g, wg, wu, wd)


def _router_kernel(x_ref, g_ref, wr_ref, e_ref, p_ref, *, n_exp):
    h = _rms(x_ref[...], g_ref[...])
    h_hi = h.astype(BF16)
    h_lo = (h - h_hi.astype(F32)).astype(BF16)
    w = wr_ref[...]
    hh = _dot(h_hi, w)
    logits = hh[:, :LANES] + hh[:, LANES:] + _dot(h_lo, w[:, :LANES])
    lane = lax.broadcasted_iota(jnp.int32, logits.shape, 1)
    logits = jnp.where(lane < n_exp, logits, -jnp.inf)
    lane_f = lane.astype(F32)
    m1 = jnp.max(logits, axis=1, keepdims=True)
    i1 = jnp.min(jnp.where(logits == m1, lane_f, float(LANES)), axis=1, keepdims=True)
    rest = jnp.where(lane_f == i1, -jnp.inf, logits)
    m2 = jnp.max(rest, axis=1, keepdims=True)
    i2 = jnp.min(jnp.where(rest == m2, lane_f, float(LANES)), axis=1, keepdims=True)
    e2 = jnp.exp(m2 - m1)
    den = 1.0 + e2
    e_ref[...] = jnp.where(lane == 0, i1, jnp.where(lane == 1, i2, 0.0)).astype(jnp.int32)
    p_ref[...] = jnp.where(lane == 0, 1.0 / den, jnp.where(lane == 1, e2 / den, 0.0))


def _router(x, g, wr, n_exp):
    n, d = x.shape
    t = _largest_tile(n, (768, 512, 256))
    return pl.pallas_call(
        functools.partial(_router_kernel, n_exp=n_exp),
        grid=(n // t,),
        in_specs=[
            pl.BlockSpec((t, d), lambda i: (i, 0)),
            pl.BlockSpec((1, d), lambda i: (0, 0)),
            pl.BlockSpec((d, 2 * LANES), lambda i: (0, 0)),
        ],
        out_specs=[pl.BlockSpec((t, LANES), lambda i: (i, 0)), pl.BlockSpec((t, LANES), lambda i: (i, 0))],
        out_shape=[jax.ShapeDtypeStruct((n, LANES), jnp.int32), jax.ShapeDtypeStruct((n, LANES), F32)],
        compiler_params=_params(("arbitrary",)),
        name="router",
    )(x, g, wr)


DMA_UNROLL = 8


def _row_copies(idx_ref, base, src_hbm, dst_ref, sem, lo, n, wait):
    assert n % 2 == 0

    def pair(r, _):
        for k in range(2):
            row = lo + 2 * r + k
            cp = pltpu.make_async_copy(src_hbm.at[pl.ds(idx_ref[base + row], 1), :], dst_ref.at[pl.ds(row, 1), :], sem)
            if wait:
                cp.wait()
            else:
                cp.start(priority=k)
        return 0

    lax.fori_loop(0, n // 2, pair, 0, unroll=DMA_UNROLL // 2)


def _moe_kernel(src_ref, blk_e_ref, n_act_ref, x_hbm, g_ref, wg_ref, wu_ref, wd_ref, o_ref, xbuf, h_ref, sem, *, nj):
    b = pl.program_id(0)
    j = pl.program_id(1)
    r = o_ref.shape[0]
    n_act = n_act_ref[0]
    slot = b % 2
    per_step = -(-r // (nj * DMA_UNROLL)) * DMA_UNROLL
    last_step = r - (nj - 1) * per_step
    assert last_step > 0

    @pl.when((b == 0) & (j == 0) & (n_act > 0))
    def _():
        _row_copies(src_ref, 0, x_hbm, xbuf.at[0], sem.at[0], 0, r, wait=False)

    @pl.when(j == 0)
    def _():
        o_ref[...] = jnp.zeros_like(o_ref)

        @pl.when(b < n_act)
        def _():
            _row_copies(src_ref, b * r, x_hbm, xbuf.at[slot], sem.at[slot], 0, r, wait=True)
            h_ref[...] = _rms(xbuf[slot], g_ref[...]).astype(BF16)

    def prefetch(n):
        _row_copies(src_ref, (b + 1) * r, x_hbm, xbuf.at[1 - slot], sem.at[1 - slot], j * per_step, n, wait=False)

    if nj > 1:
        pl.when((b + 1 < n_act) & (j < nj - 1))(lambda: prefetch(per_step))
    pl.when((b + 1 < n_act) & (j == nj - 1))(lambda: prefetch(last_step))

    @pl.when(b < n_act)
    def _():
        o_ref[...] += _swiglu(h_ref[...], wg_ref[0], wu_ref[0], wd_ref[0])


def _moe_experts(x, src_tok, g, wg, wu, wd, blk_e, n_act):
    d = x.shape[1]
    n_rows = src_tok.shape[0]
    dff = wg.shape[2]
    r = EXPERT_BLOCK
    tf = _largest_tile(dff, (1024, 512, 256, 128))
    nj = dff // tf

    def wmap_in(b, j, src, blk_e, n_act):
        return (blk_e[b], 0, jnp.where(b < n_act[0], j, nj - 1))

    def wmap_dn(b, j, src, blk_e, n_act):
        return (blk_e[b], jnp.where(b < n_act[0], j, nj - 1), 0)

    return pl.pallas_call(
        functools.partial(_moe_kernel, nj=nj),
        grid_spec=pltpu.PrefetchScalarGridSpec(
            num_scalar_prefetch=3,
            grid=(n_rows // r, nj),
            in_specs=[
                pl.BlockSpec(memory_space=pl.ANY),
                pl.BlockSpec((1, d), lambda b, j, *_: (0, 0)),
                pl.BlockSpec((1, d, tf), wmap_in),
                pl.BlockSpec((1, d, tf), wmap_in),
                pl.BlockSpec((1, tf, d), wmap_dn),
            ],
            out_specs=pl.BlockSpec((r, d), lambda b, j, *_: (b, 0)),
            scratch_shapes=[pltpu.VMEM((2, r, d), F32), pltpu.VMEM((r, d), BF16), pltpu.SemaphoreType.DMA((2,))],
        ),
        out_shape=jax.ShapeDtypeStruct((n_rows, d), F32),
        compiler_params=_params(("arbitrary", "arbitrary")),
        name="moe_experts",
    )(src_tok, blk_e, n_act, x, g, wg, wu, wd)


def _combine_kernel(pos_ref, x_ref, p_ref, fn_ref, y_hbm, o_ref, y0_ref, y1_ref, sem, *, tile_of, final_norm):
    t = x_ref.shape[0]
    base = tile_of(pl.program_id(0)) * t * TOP_K

    def copies(r):
        c0 = pltpu.make_async_copy(y_hbm.at[pl.ds(pos_ref[base + TOP_K * r], 1), :], y0_ref.at[pl.ds(r, 1), :], sem)
        c1 = pltpu.make_async_copy(y_hbm.at[pl.ds(pos_ref[base + TOP_K * r + 1], 1), :], y1_ref.at[pl.ds(r, 1), :], sem)
        return c0, c1

    def start(r, _):
        for k, c in enumerate(copies(r)):
            c.start(priority=k)
        return 0

    def wait(r, _):
        for c in copies(r):
            c.wait()
        return 0

    lax.fori_loop(0, t, start, 0, unroll=DMA_UNROLL)
    lax.fori_loop(0, t, wait, 0, unroll=DMA_UNROLL)
    p = p_ref[...]
    y = x_ref[...] + (y0_ref[...] * p[:, 0:1] + y1_ref[...] * p[:, 1:2])
    if final_norm:
        y = _rms(y, fn_ref[...])
    o_ref[...] = y


def _combine(x, gates, pos, yb, fn, n_out_tiles, tile_of, final_norm):
    d = x.shape[1]
    t = ALIGN
    kern = functools.partial(_combine_kernel, tile_of=tile_of, final_norm=final_norm)
    return pl.pallas_call(
        kern,
        grid_spec=pltpu.PrefetchScalarGridSpec(
            num_scalar_prefetch=1,
            grid=(n_out_tiles,),
            in_specs=[
                pl.BlockSpec((t, d), lambda i, pos: (tile_of(i), 0)),
                pl.BlockSpec((t, LANES), lambda i, pos: (tile_of(i), 0)),
                pl.BlockSpec((1, d), lambda i, pos: (0, 0)),
                pl.BlockSpec(memory_space=pl.ANY),
            ],
            out_specs=pl.BlockSpec((t, d), lambda i, pos: (i, 0)),
            scratch_shapes=[pltpu.VMEM((t, d), F32), pltpu.VMEM((t, d), F32), pltpu.SemaphoreType.DMA],
        ),
        out_shape=jax.ShapeDtypeStruct((n_out_tiles * t, d), F32),
        compiler_params=_params(("arbitrary",)),
        name="moe_combine",
    )(pos, x, gates, fn, yb)


def _norm_kernel(x_ref, g_ref, o_ref):
    o_ref[...] = _rms(x_ref[...], g_ref[...])


def _final_norm(x, g, n_out_tiles, tile_of):
    d = x.shape[1]
    t = ALIGN
    return pl.pallas_call(
        _norm_kernel,
        grid=(n_out_tiles,),
        in_specs=[pl.BlockSpec((t, d), lambda i: (tile_of(i), 0)), pl.BlockSpec((1, d), lambda i: (0, 0))],
        out_specs=pl.BlockSpec((t, d), lambda i: (i, 0)),
        out_shape=jax.ShapeDtypeStruct((n_out_tiles * t, d), F32),
        compiler_params=_params(("arbitrary",)),
        name="final_norm",
    )(x, g)


def _routing_tables(top_e, valid, n_exp):
    n = top_e.shape[0]
    a = n * TOP_K
    e_flat = jnp.where(valid[:, None], top_e, n_exp).reshape(-1)
    onehot = (e_flat[:, None] == jnp.arange(n_exp, dtype=jnp.int32)[None, :]).astype(jnp.int32)
    csum = jnp.cumsum(onehot, axis=0)
    counts = csum[-1]
    rank = jnp.sum((csum - onehot) * onehot, axis=1)
    padded = (counts + EXPERT_BLOCK - 1) // EXPERT_BLOCK * EXPERT_BLOCK
    pad_end = jnp.cumsum(padded)
    pad_start = pad_end - padded
    n_blocks = -(-a // EXPERT_BLOCK) + n_exp
    n_rows = n_blocks * EXPERT_BLOCK
    e_safe = jnp.minimum(e_flat, n_exp - 1)
    routed = e_flat < n_exp
    dest = jnp.where(routed, pad_start[e_safe] + rank, n_rows)
    tok = jnp.arange(a, dtype=jnp.int32) // TOP_K
    src_tok = jnp.zeros((n_rows,), jnp.int32).at[dest].set(tok, mode="drop")
    blk_start = jnp.arange(n_blocks, dtype=jnp.int32) * EXPERT_BLOCK
    blk_e = jnp.minimum(jnp.sum(pad_end[None, :] <= blk_start[:, None], axis=1), n_exp - 1).astype(jnp.int32)
    n_act = (pad_end[-1] // EXPERT_BLOCK).astype(jnp.int32).reshape(1)
    last_e = blk_e[jnp.maximum(n_act[0] - 1, 0)]
    blk_e = jnp.where(jnp.arange(n_blocks) < n_act[0], blk_e, last_e)
    pos = jnp.where(routed, dest, 0).astype(jnp.int32)
    return src_tok, blk_e, n_act, pos


def _rope_tables(n_rows, first_valid):
    inv = ROPE_THETA ** (-jnp.arange(0, ROPE_DIM, 2, dtype=F32) / ROPE_DIM)
    pos = (jnp.arange(n_rows, dtype=jnp.int32) - first_valid).astype(F32)
    ang = pos[:, None] * inv[None, :]
    cos, sin = jnp.cos(ang), jnp.sin(ang)
    zeros = jnp.zeros((n_rows, LANES - ROPE_DIM), F32)
    return jnp.concatenate([cos, cos, zeros], axis=1), jnp.concatenate([-sin, sin, zeros], axis=1)


def kernel(x_prompt, x_sample, meta_tokens, mix_norm, w_in, q_norm, w_uq, kv_norm, w_ukv, pool_w, pool_scale, w_out, ffn_norm, dense_w_gate, dense_w_up, dense_w_down, moe_w_router, moe_w_gate, moe_w_up, moe_w_down, final_norm):
    depth, d = mix_norm.shape
    n_meta = meta_tokens.shape[0]
    ql = q_norm.shape[1]
    kvl = kv_norm.shape[1]
    dp = pool_scale.shape[1]
    n_exp = moe_w_router.shape[2]
    da = N_HEADS * V_DIM
    assert d == da + dp and dp % (len(POOL_WINDOWS) * LANES) == 0
    assert ql % LANES == 0 and kvl % LANES == 0 and n_meta <= LANES and n_meta % 8 == 0
    assert w_in.shape[2] == ql + kvl + ROPE_DIM + dp and n_exp <= LANES
    dims = (d, ql, kvl, dp)
    first_valid = ALIGN - n_meta

    groups = [(x_prompt.shape[0], x_prompt.shape[1]), (x_sample.shape[0], x_sample.shape[1])]
    for _, s_len in groups:
        assert s_len % ALIGN == 0
    lead = jnp.concatenate([jnp.zeros((first_valid, d), F32), meta_tokens.astype(F32)], axis=0)
    parts = []
    for xs in (x_prompt, x_sample):
        for bi in range(xs.shape[0]):
            parts += [lead, xs[bi]]
    x = jnp.concatenate(parts, axis=0)
    n_tok = x.shape[0]
    valid = jnp.concatenate(
        [jnp.tile(jnp.arange(ALIGN + s_len) >= first_valid, nb) for nb, s_len in groups])

    tables = [_rope_tables(ALIGN + s_len, first_valid) for _, s_len in groups]

    for l in range(depth):
        wi = w_in[l]
        c0, c1, c2 = ql, ql + kvl, ql + kvl + ROPE_DIM
        win = jnp.concatenate(
            [wi[:, :c1], wi[:, c2:], wi[:, c1:c2], jnp.zeros((d, LANES - ROPE_DIM), F32)], axis=1).astype(BF16)
        wq3 = w_uq[l].reshape(ql, N_HEADS, QK_DIM)
        wq = jnp.concatenate([wq3, jnp.zeros((ql, N_HEADS, HEAD_PAD - QK_DIM), F32)], axis=2)
        wq = wq.reshape(ql, N_HEADS * HEAD_PAD).astype(BF16)
        wkv3 = w_ukv[l].reshape(kvl, N_HEADS, NOPE_DIM + V_DIM)
        wk = wkv3[:, :, :NOPE_DIM].reshape(kvl, N_HEADS * NOPE_DIM).astype(BF16)
        wvt = wkv3[:, :, NOPE_DIM:].reshape(kvl, N_HEADS * V_DIM).T.astype(BF16)
        woa = w_out[l][:da].astype(BF16)
        wop = w_out[l][da:].astype(BF16)
        pw = pool_w[l].astype(BF16)

        tile_off = 0
        for (nb, s_len), (cos_t, sin_t) in zip(groups, tables):
            tps = (ALIGN + s_len) // ALIGN
            n_tiles = nb * tps
            q, k, vt, u = _in_proj(x, tile_off, n_tiles, tps, mix_norm[l][None], win, q_norm[l][None], wq,
                                   kv_norm[l][None], wk, wvt, cos_t, sin_t, dims)
            attn = _attention(q, k, vt, nb, s_len, first_valid)
            x = _out_proj(x, attn, u, tile_off, n_tiles, tps, pw, pool_scale[l][None], woa, wop, dims,
                          first_valid, n_meta + s_len)
            tile_off += n_tiles

        last = l == depth - 1
        if l % 2 == 0:
            i = l // 2
            x = _dense_ffn(x, ffn_norm[l][None], dense_w_gate[i].astype(BF16), dense_w_up[i].astype(BF16),
                           dense_w_down[i].astype(BF16))
            moe = None
        else:
            i = l // 2
            wr = jnp.concatenate([moe_w_router[i], jnp.zeros((d, LANES - n_exp), F32)], axis=1)
            wr_hi = wr.astype(BF16)
            wr_lo = (wr - wr_hi.astype(F32)).astype(BF16)
            top_e, gates = _router(x, ffn_norm[l][None], jnp.concatenate([wr_hi, wr_lo], axis=1), n_exp)
            src_tok, blk_e, n_act, pos = _routing_tables(top_e[:, :TOP_K], valid, n_exp)
            yb = _moe_experts(x, src_tok, ffn_norm[l][None], moe_w_gate[i].astype(BF16), moe_w_up[i].astype(BF16),
                              moe_w_down[i].astype(BF16), blk_e, n_act)
            moe = (gates, pos, yb)
            if not last:
                x = _combine(x, gates, pos, yb, final_norm[None], n_tok // ALIGN, lambda i: i, False)
                moe = None

    outs = []
    tile_off = 0
    for (nb, s_len), xs in zip(groups, (x_prompt, x_sample)):
        real = s_len // ALIGN

        def tile_of(i, tile_off=tile_off, real=real):
            return tile_off + (i // real) * (real + 1) + 1 + i % real

        if moe is not None:
            gates, pos, yb = moe
            y = _combine(x, gates, pos, yb, final_norm[None], nb * real, tile_of, True)
        else:
            y = _final_norm(x, final_norm[None], nb * real, tile_of)
        outs.append(y.reshape(nb, s_len, d).astype(xs.dtype))
        tile_off += nb * (real + 1)
    return tuple(outs)
```

```python
import functools

import jax
import jax.numpy as jnp
from jax import lax
from jax.experimental import pallas as pl
from jax.experimental.pallas import tpu as pltpu

N_HEADS = 8
NOPE_DIM = 128
ROPE_DIM = 64
V_DIM = 128
QK_DIM = NOPE_DIM + ROPE_DIM
HEAD_PAD = 256
V_PAD = V_DIM + 16
POOL_WINDOWS = (2, 4, 8, 16)
HALO = 8
TOP_K = 2
EXPERT_BLOCK = 512
ROPE_THETA = 10000.0
RMS_EPS = 1e-6
LOG2E = 1.4426950408889634

LANES = 128
ALIGN = 256
LOOKAHEAD = 2
MAX_Q_UNITS = 5
MAX_Q_UNITS_WHOLE = 17
NEG = -1e30
VMEM_LIMIT = 60 * 1024 * 1024

F32 = jnp.float32
BF16 = jnp.bfloat16


def _rms(x, g):
    ms = jnp.mean(x * x, axis=-1, keepdims=True)
    return x * lax.rsqrt(ms + RMS_EPS) * g


def _dot(a, b):
    return jnp.dot(a, b, preferred_element_type=F32)


def _dot_nt(a, b):
    return lax.dot_general(a, b, (((1,), (1,)), ((), ())), preferred_element_type=F32)


def _params(sem, flags=None):
    return pltpu.CompilerParams(dimension_semantics=sem, vmem_limit_bytes=VMEM_LIMIT, flags=flags)


def _largest_tile(n, candidates):
    for c in candidates:
        if n % c == 0:
            return c
    raise ValueError(f"no tile in {candidates} divides {n}")


class _RowSource:
    def __init__(self, rows, lead, tile_off, tiles_per_seq, from_inputs):
        self.rows, self.lead, self.from_inputs, self.tiles_per_seq = rows, lead, from_inputs, tiles_per_seq
        real = tiles_per_seq - 1
        if from_inputs:
            self.index = lambda i: ((i // tiles_per_seq) * real + jnp.maximum(i % tiles_per_seq - 1, 0), 0)
        else:
            self.index = lambda i: (i + tile_off, 0)

    def specs(self, t, d):
        return [pl.BlockSpec((t, d), self.index), pl.BlockSpec((ALIGN, d), lambda i: (0, 0))]

    def load(self, x_ref, lead_ref):
        if not self.from_inputs:
            return x_ref[...]
        return jnp.where(pl.program_id(0) % self.tiles_per_seq == 0, lead_ref[...], x_ref[...])


def _in_kernel(x_ref, lead_ref, g_ref, win_ref, qn_ref, wq_ref, kvn_ref, wk_ref, wvt_ref, cos_ref, sin_ref,
               q_ref, k_ref, vt_ref, u_ref, *, src, ql, kvl, dp, scale):
    h = _rms(src.load(x_ref, lead_ref), g_ref[...]).astype(BF16)
    z = _dot(h, win_ref[...])
    cq = z[:, :ql]
    ckv = z[:, ql:ql + kvl]
    u_ref[...] = z[:, ql + kvl:ql + kvl + dp]
    kr = z[:, ql + kvl + dp:]
    cqn = _rms(cq, qn_ref[...]).astype(BF16)
    ckvn = _rms(ckv, kvn_ref[...]).astype(BF16)
    q = _dot(cqn, wq_ref[...])
    kn = _dot(ckvn, wk_ref[...])
    vt = _dot_nt(wvt_ref[...], ckvn).astype(BF16)
    ones = jnp.ones((V_PAD - V_DIM, vt.shape[1]), BF16)
    for hh in range(N_HEADS):
        vt_ref[hh * V_PAD:hh * V_PAD + V_DIM, :] = vt[hh * V_DIM:(hh + 1) * V_DIM]
        vt_ref[hh * V_PAD + V_DIM:(hh + 1) * V_PAD, :] = ones

    c = cos_ref[...]
    s = sin_ref[...]
    first_half = lax.broadcasted_iota(jnp.int32, c.shape, 1) < ROPE_DIM // 2

    def rope(v):
        partner = jnp.where(first_half, pltpu.roll(v, LANES - ROPE_DIM // 2, 1),
                            pltpu.roll(v, ROPE_DIM // 2, 1))
        return v * c + partner * s

    kr_r = rope(kr).astype(BF16)
    for hh in range(N_HEADS):
        lo = hh * HEAD_PAD
        q_ref[:, lo:lo + NOPE_DIM] = (q[:, lo:lo + NOPE_DIM] * scale).astype(BF16)
        q_ref[:, lo + NOPE_DIM:lo + HEAD_PAD] = (rope(q[:, lo + NOPE_DIM:lo + HEAD_PAD]) * scale).astype(BF16)
        k_ref[:, lo:lo + NOPE_DIM] = kn[:, hh * NOPE_DIM:(hh + 1) * NOPE_DIM].astype(BF16)
        k_ref[:, lo + NOPE_DIM:lo + HEAD_PAD] = kr_r


def _in_proj(src, n_tiles, g, win, qn, wq, kvn, wk, wvt, cos_t, sin_t, dims):
    d, ql, kvl, dp = dims
    t = ALIGN
    rows = n_tiles * t
    zw = win.shape[1]
    tiles_per_seq = src.tiles_per_seq
    const = lambda i: (0, 0)
    kern = functools.partial(_in_kernel, src=src, ql=ql, kvl=kvl, dp=dp, scale=QK_DIM ** -0.5 * LOG2E)
    return pl.pallas_call(
        kern,
        grid=(n_tiles,),
        in_specs=src.specs(t, d) + [
            pl.BlockSpec((1, d), const),
            pl.BlockSpec((d, zw), const),
            pl.BlockSpec((1, ql), const),
            pl.BlockSpec((ql, N_HEADS * HEAD_PAD), const),
            pl.BlockSpec((1, kvl), const),
            pl.BlockSpec((kvl, N_HEADS * NOPE_DIM), const),
            pl.BlockSpec((N_HEADS * V_DIM, kvl), const),
            pl.BlockSpec((t, LANES), lambda i: (i % tiles_per_seq, 0)),
            pl.BlockSpec((t, LANES), lambda i: (i % tiles_per_seq, 0)),
        ],
        out_specs=[
            pl.BlockSpec((t, N_HEADS * HEAD_PAD), lambda i: (i, 0)),
            pl.BlockSpec((t, N_HEADS * HEAD_PAD), lambda i: (i, 0)),
            pl.BlockSpec((N_HEADS * V_PAD, t), lambda i: (0, i)),
            pl.BlockSpec((t, dp), lambda i: (i, 0)),
        ],
        out_shape=[
            jax.ShapeDtypeStruct((rows, N_HEADS * HEAD_PAD), BF16),
            jax.ShapeDtypeStruct((rows, N_HEADS * HEAD_PAD), BF16),
            jax.ShapeDtypeStruct((N_HEADS * V_PAD, rows), BF16),
            jax.ShapeDtypeStruct((rows, dp), F32),
        ],
        compiler_params=_params(("arbitrary",)),
        name="in_proj",
    )(src.rows, src.lead, g, win, qn, wq, kvn, wk, wvt, cos_t, sin_t)


def _attn_kernel(q_ref, k_ref, vt_ref, o_ref, *, s_len, tk, n_sub, first_valid):
    q = q_ref[...]
    tq = q.shape[0]

    def update(s, vtb, carry):
        m, acc = carry
        m_new = jnp.maximum(m, jnp.max(s, axis=0, keepdims=True))
        p = jnp.exp2(s - m_new).astype(BF16)
        acc = jnp.exp2(m - m_new) * acc + _dot(vtb, p)
        return m_new, acc

    def score(kb):
        return _dot_nt(kb, q)

    carry = (jnp.full((1, tq), NEG, F32), jnp.zeros((V_PAD, tq), F32))
    lo = ALIGN - LANES
    row = lax.broadcasted_iota(jnp.int32, (LANES, tq), 0) + lo
    s_lead = jnp.where(row >= first_valid, _dot_nt(k_ref[lo:ALIGN, :], q), NEG)
    carry = update(s_lead, vt_ref[:, lo:ALIGN], carry)

    def body(c, carry):
        offs = [pl.multiple_of(ALIGN + (c * n_sub + j) * tk, LANES) for j in range(n_sub)]
        scores = [score(k_ref[pl.ds(off, tk), :]) for off in offs[:LOOKAHEAD]]
        for j, off in enumerate(offs):
            if j + LOOKAHEAD < n_sub:
                scores.append(score(k_ref[pl.ds(offs[j + LOOKAHEAD], tk), :]))
            carry = update(scores[j], vt_ref[:, pl.ds(off, tk)], carry)
        return carry

    _, acc = lax.fori_loop(0, s_len // (tk * n_sub), body, carry)
    o_ref[...] = (acc[:V_DIM] / acc[V_DIM:V_DIM + 1]).T.astype(BF16)


def _attention(q, k, vt, n_seq, s_len, first_valid):
    region = ALIGN + s_len
    units = region // ALIGN
    q_units = units if units <= MAX_Q_UNITS_WHOLE else max(u for u in range(1, MAX_Q_UNITS + 1) if units % u == 0)
    tq = q_units * ALIGN
    nq = units // q_units
    tk = _largest_tile(s_len, (512, 256))
    n_sub = _largest_tile(s_len // tk, (8, 4, 2, 1) if q_units <= MAX_Q_UNITS else (2, 1))
    kern = functools.partial(_attn_kernel, s_len=s_len, tk=tk, n_sub=n_sub, first_valid=first_valid)
    return pl.pallas_call(
        kern,
        grid=(n_seq, N_HEADS, nq),
        in_specs=[
            pl.BlockSpec((tq, HEAD_PAD), lambda s, h, i: (s * nq + i, h)),
            pl.BlockSpec((region, HEAD_PAD), lambda s, h, i: (s, h)),
            pl.BlockSpec((V_PAD, region), lambda s, h, i: (h, s)),
        ],
        out_specs=pl.BlockSpec((tq, V_DIM), lambda s, h, i: (s * nq + i, h)),
        out_shape=jax.ShapeDtypeStruct((n_seq * region, N_HEADS * V_DIM), BF16),
        compiler_params=_params(("arbitrary", "arbitrary", "arbitrary")),
        name="attention",
    )(q, k, vt)


def _out_kernel(attn_ref, u_ref, uprev_ref, unext_ref, x_ref, lead_ref, pw_ref, ps_ref, woa_ref, wop_ref, *rest,
                src, first_valid, n_pos):
    o_ref = rest[-1]
    t, dp = u_ref.shape
    pg = dp // len(POOL_WINDOWS)
    p0 = (pl.program_id(0) % src.tiles_per_seq) * t - first_valid
    u = u_ref[...]
    ucat = jnp.concatenate([uprev_ref[...], u, unext_ref[...]], axis=0).astype(BF16)
    tt = lax.broadcasted_iota(jnp.int32, (t, t + 2 * HALO), 0)
    jj = lax.broadcasted_iota(jnp.int32, (t, t + 2 * HALO), 1)
    rel = jj - HALO - tt
    pj = p0 - HALO + jj
    col_ok = (pj >= 0) & (pj < n_pos)
    pos = p0 + lax.broadcasted_iota(jnp.int32, (t, 1), 0)
    ps = ps_ref[...]
    pooled = []
    for gi, w in enumerate(POOL_WINDOWS):
        band = jnp.where((rel >= -(w // 2)) & (rel <= w // 2 - 1) & col_ok, 1.0, 0.0).astype(BF16)
        wsum = _dot(band, ucat[:, gi * pg:(gi + 1) * pg])
        cnt = jnp.minimum(pos + w // 2, n_pos) - jnp.maximum(pos - w // 2, 0)
        cnt = jnp.maximum(cnt, 1).astype(F32)
        centred = (wsum / cnt - u[:, gi * pg:(gi + 1) * pg]).astype(BF16)
        pooled.append(_dot(centred, pw_ref[gi]) * ps[:, gi * pg:(gi + 1) * pg])
    pool = jnp.concatenate(pooled, axis=1).astype(BF16)
    y = _dot(attn_ref[...], woa_ref[...]) + _dot(pool, wop_ref[...])
    o_ref[...] = src.load(x_ref, lead_ref) + y


def _out_proj(src, dst, n_tok, attn, u, tile_off, n_tiles, pw, ps, woa, wop, dims, first_valid, n_pos):
    d, _, _, dp = dims
    t = ALIGN
    pg = dp // len(POOL_WINDOWS)
    da = N_HEADS * V_DIM
    hb = t // HALO
    last = n_tiles * hb - 1
    const = lambda i: (0, 0)
    kern = functools.partial(_out_kernel, src=src, first_valid=first_valid, n_pos=n_pos)
    operands = [attn, u, u, u, src.rows, src.lead, pw, ps, woa, wop]
    in_specs = [
        pl.BlockSpec((t, da), lambda i: (i, 0)),
        pl.BlockSpec((t, dp), lambda i: (i, 0)),
        pl.BlockSpec((HALO, dp), lambda i: (jnp.maximum(i * hb - 1, 0), 0)),
        pl.BlockSpec((HALO, dp), lambda i: (jnp.minimum((i + 1) * hb, last), 0)),
    ] + src.specs(t, d) + [
        pl.BlockSpec((len(POOL_WINDOWS), pg, pg), lambda i: (0, 0, 0)),
        pl.BlockSpec((1, dp), const),
        pl.BlockSpec((da, d), const),
        pl.BlockSpec((dp, d), const),
    ]
    if not src.from_inputs:
        aliases = {4: 0}
    elif dst is None:
        aliases = {}
    else:
        operands.append(dst)
        in_specs.append(pl.BlockSpec(memory_space=pl.ANY))
        aliases = {len(operands) - 1: 0}
    return pl.pallas_call(
        kern,
        grid=(n_tiles,),
        in_specs=in_specs,
        out_specs=pl.BlockSpec((t, d), lambda i: (i + tile_off, 0)),
        out_shape=jax.ShapeDtypeStruct((n_tok, d), F32),
        input_output_aliases=aliases,
        compiler_params=_params(("arbitrary",)),
        name="out_proj",
    )(*operands)


def _swiglu(h, wg, wu, wd):
    a = _dot(h, wg)
    b = _dot(h, wu)
    act = a * (1.0 / (1.0 + jnp.exp(-a))) * b
    return _dot(act.astype(wd.dtype), wd)


def _ffn_kernel(x_ref, g_ref, wg_ref, wu_ref, wd_ref, o_ref, h_ref):
    @pl.when(pl.program_id(1) == 0)
    def _():
        x = x_ref[...]
        h_ref[...] = _rms(x, g_ref[...]).astype(BF16)
        o_ref[...] = x

    o_ref[...] += _swiglu(h_ref[...], wg_ref[...], wu_ref[...], wd_ref[...])


def _dense_ffn(x, g, wg, wu, wd):
    n, d = x.shape
    dff = wg.shape[1]
    t = _largest_tile(n, (768, 512, 256))
    tf = _largest_tile(dff, (512, 256, 128))
    return pl.pallas_call(
        _ffn_kernel,
        grid=(n // t, dff // tf),
        in_specs=[
            pl.BlockSpec((t, d), lambda i, j: (i, 0)),
            pl.BlockSpec((1, d), lambda i, j: (0, 0)),
            pl.BlockSpec((d, tf), lambda i, j: (0, j)),
            pl.BlockSpec((d, tf), lambda i, j: (0, j)),
            pl.BlockSpec((tf, d), lambda i, j: (j, 0)),
        ],
        out_specs=pl.BlockSpec((t, d), lambda i, j: (i, 0)),
        out_shape=jax.ShapeDtypeStruct((n, d), F32),
        scratch_shapes=[pltpu.VMEM((t, d), BF16)],
        compiler_params=_params(("arbitrary", "arbitrary")),
        name="dense_ffn",
    )(x, g, wg, wu, wd)


def _router_kernel(x_ref, g_ref, wr_ref, e_ref, p_ref, *, n_exp):
    h = _rms(x_ref[...], g_ref[...])
    h_hi = h.astype(BF16)
    h_lo = (h - h_hi.astype(F32)).astype(BF16)
    w = wr_ref[...]
    hh = _dot(h_hi, w)
    logits = hh[:, :LANES] + hh[:, LANES:] + _dot(h_lo, w[:, :LANES])
    lane = lax.broadcasted_iota(jnp.int32, logits.shape, 1)
    logits = jnp.where(lane < n_exp, logits, -jnp.inf)
    lane_f = lane.astype(F32)
    m1 = jnp.max(logits, axis=1, keepdims=True)
    i1 = jnp.min(jnp.where(logits == m1, lane_f, float(LANES)), axis=1, keepdims=True)
    rest = jnp.where(lane_f == i1, -jnp.inf, logits)
    m2 = jnp.max(rest, axis=1, keepdims=True)
    i2 = jnp.min(jnp.where(rest == m2, lane_f, float(LANES)), axis=1, keepdims=True)
    e2 = jnp.exp(m2 - m1)
    den = 1.0 + e2
    e_ref[...] = jnp.where(lane == 0, i1, jnp.where(lane == 1, i2, 0.0)).astype(jnp.int32)
    p_ref[...] = jnp.where(lane == 0, 1.0 / den, jnp.where(lane == 1, e2 / den, 0.0))


def _router(x, g, wr, n_exp):
    n, d = x.shape
    t = _largest_tile(n, (768, 512, 256))
    return pl.pallas_call(
        functools.partial(_router_kernel, n_exp=n_exp),
        grid=(n // t,),
        in_specs=[
            pl.BlockSpec((t, d), lambda i: (i, 0)),
            pl.BlockSpec((1, d), lambda i: (0, 0)),
            pl.BlockSpec((d, 2 * LANES), lambda i: (0, 0)),
        ],
        out_specs=[pl.BlockSpec((t, LANES), lambda i: (i, 0)), pl.BlockSpec((t, LANES), lambda i: (i, 0))],
        out_shape=[jax.ShapeDtypeStruct((n, LANES), jnp.int32), jax.ShapeDtypeStruct((n, LANES), F32)],
        compiler_params=_params(("arbitrary",)),
        name="router",
    )(x, g, wr)


DMA_UNROLL = 8


def _row_copies(idx_ref, base, src_hbm, dst_ref, sem, lo, n, wait):
    assert n % 2 == 0

    def pair(r, _):
        for k in range(2):
            row = lo + 2 * r + k
            cp = pltpu.make_async_copy(src_hbm.at[pl.ds(idx_ref[base + row], 1), :], dst_ref.at[pl.ds(row, 1), :], sem)
            if wait:
                cp.wait()
            else:
                cp.start(priority=k)
        return 0

    lax.fori_loop(0, n // 2, pair, 0, unroll=DMA_UNROLL // 2)


def _moe_kernel(src_ref, blk_e_ref, n_act_ref, x_hbm, g_ref, wg_ref, wu_ref, wd_ref, o_ref, xbuf, h_ref, sem, *, nj):
    b = pl.program_id(0)
    j = pl.program_id(1)
    r = o_ref.shape[0]
    n_act = n_act_ref[0]
    slot = b % 2
    per_step = -(-r // (nj * DMA_UNROLL)) * DMA_UNROLL
    last_step = r - (nj - 1) * per_step
    assert last_step > 0

    @pl.when((b == 0) & (j == 0) & (n_act > 0))
    def _():
        _row_copies(src_ref, 0, x_hbm, xbuf.at[0], sem.at[0], 0, r, wait=False)

    @pl.when(j == 0)
    def _():
        o_ref[...] = jnp.zeros_like(o_ref)

        @pl.when(b < n_act)
        def _():
            _row_copies(src_ref, b * r, x_hbm, xbuf.at[slot], sem.at[slot], 0, r, wait=True)
            h_ref[...] = _rms(xbuf[slot], g_ref[...]).astype(BF16)

    def prefetch(n):
        _row_copies(src_ref, (b + 1) * r, x_hbm, xbuf.at[1 - slot], sem.at[1 - slot], j * per_step, n, wait=False)

    if nj > 1:
        pl.when((b + 1 < n_act) & (j < nj - 1))(lambda: prefetch(per_step))
    pl.when((b + 1 < n_act) & (j == nj - 1))(lambda: prefetch(last_step))

    @pl.when(b < n_act)
    def _():
        o_ref[...] += _swiglu(h_ref[...], wg_ref[0], wu_ref[0], wd_ref[0])


def _moe_experts(x, src_tok, g, wg, wu, wd, blk_e, n_act):
    d = x.shape[1]
    n_rows = src_tok.shape[0]
    dff = wg.shape[2]
    r = EXPERT_BLOCK
    tf = _largest_tile(dff, (1024, 512, 256, 128))
    nj = dff // tf

    def wmap_in(b, j, src, blk_e, n_act):
        return (blk_e[b], 0, jnp.where(b < n_act[0], j, nj - 1))

    def wmap_dn(b, j, src, blk_e, n_act):
        return (blk_e[b], jnp.where(b < n_act[0], j, nj - 1), 0)

    return pl.pallas_call(
        functools.partial(_moe_kernel, nj=nj),
        grid_spec=pltpu.PrefetchScalarGridSpec(
            num_scalar_prefetch=3,
            grid=(n_rows // r, nj),
            in_specs=[
                pl.BlockSpec(memory_space=pl.ANY),
                pl.BlockSpec((1, d), lambda b, j, *_: (0, 0)),
                pl.BlockSpec((1, d, tf), wmap_in),
                pl.BlockSpec((1, d, tf), wmap_in),
                pl.BlockSpec((1, tf, d), wmap_dn),
            ],
            out_specs=pl.BlockSpec((r, d), lambda b, j, *_: (b, 0)),
            scratch_shapes=[pltpu.VMEM((2, r, d), F32), pltpu.VMEM((r, d), BF16), pltpu.SemaphoreType.DMA((2,))],
        ),
        out_shape=jax.ShapeDtypeStruct((n_rows, d), F32),
        compiler_params=_params(("arbitrary", "arbitrary")),
        name="moe_experts",
    )(src_tok, blk_e, n_act, x, g, wg, wu, wd)


def _combine_kernel(pos_ref, x_ref, p_ref, fn_ref, y_hbm, o_ref, y_ref, sem, *, tile_of, final_norm):
    t = x_ref.shape[0]
    i = pl.program_id(0)

    def tile_copies(tile, slot, wait):
        base = tile_of(tile) * t * TOP_K

        def one(r, _):
            for k in range(TOP_K):
                cp = pltpu.make_async_copy(y_hbm.at[pl.ds(pos_ref[base + TOP_K * r + k], 1), :],
                                           y_ref.at[slot, k, pl.ds(r, 1), :], sem.at[slot])
                if wait:
                    cp.wait()
                else:
                    cp.start(priority=k)
            return 0

        lax.fori_loop(0, t, one, 0, unroll=DMA_UNROLL // TOP_K)

    slot = i % 2
    pl.when(i == 0)(lambda: tile_copies(i, 0, wait=False))
    pl.when(i + 1 < pl.num_programs(0))(lambda: tile_copies(i + 1, 1 - slot, wait=False))
    tile_copies(i, slot, wait=True)
    p = p_ref[...]
    y = x_ref[...] + (y_ref[slot, 0] * p[:, 0:1] + y_ref[slot, 1] * p[:, 1:2])
    if final_norm:
        y = _rms(y, fn_ref[...])
    o_ref[...] = y


def _combine(x, gates, pos, yb, fn, n_out_tiles, tile_of, final_norm):
    d = x.shape[1]
    t = ALIGN
    kern = functools.partial(_combine_kernel, tile_of=tile_of, final_norm=final_norm)
    return pl.pallas_call(
        kern,
        grid_spec=pltpu.PrefetchScalarGridSpec(
            num_scalar_prefetch=1,
            grid=(n_out_tiles,),
            in_specs=[
                pl.BlockSpec((t, d), lambda i, pos: (tile_of(i), 0)),
                pl.BlockSpec((t, LANES), lambda i, pos: (tile_of(i), 0)),
                pl.BlockSpec((1, d), lambda i, pos: (0, 0)),
                pl.BlockSpec(memory_space=pl.ANY),
            ],
            out_specs=pl.BlockSpec((t, d), lambda i, pos: (i, 0)),
            scratch_shapes=[pltpu.VMEM((2, TOP_K, t, d), F32), pltpu.SemaphoreType.DMA((2,))],
        ),
        out_shape=jax.ShapeDtypeStruct((n_out_tiles * t, d), F32),
        compiler_params=_params(("arbitrary",)),
        name="moe_combine",
    )(pos, x, gates, fn, yb)


def _norm_kernel(x_ref, g_ref, o_ref):
    o_ref[...] = _rms(x_ref[...], g_ref[...])


def _final_norm(x, g, n_out_tiles, tile_of):
    d = x.shape[1]
    t = ALIGN
    return pl.pallas_call(
        _norm_kernel,
        grid=(n_out_tiles,),
        in_specs=[pl.BlockSpec((t, d), lambda i: (tile_of(i), 0)), pl.BlockSpec((1, d), lambda i: (0, 0))],
        out_specs=pl.BlockSpec((t, d), lambda i: (i, 0)),
        out_shape=jax.ShapeDtypeStruct((n_out_tiles * t, d), F32),
        compiler_params=_params(("arbitrary",)),
        name="final_norm",
    )(x, g)


def _routing_tables(top_e, valid, n_exp):
    n = top_e.shape[0]
    a = n * TOP_K
    e_flat = jnp.where(valid[:, None], top_e, n_exp).reshape(-1)
    onehot = (e_flat[:, None] == jnp.arange(n_exp, dtype=jnp.int32)[None, :]).astype(jnp.int32)
    csum = jnp.cumsum(onehot, axis=0)
    counts = csum[-1]
    rank = jnp.sum((csum - onehot) * onehot, axis=1)
    padded = (counts + EXPERT_BLOCK - 1) // EXPERT_BLOCK * EXPERT_BLOCK
    pad_end = jnp.cumsum(padded)
    pad_start = pad_end - padded
    n_blocks = -(-a // EXPERT_BLOCK) + n_exp
    n_rows = n_blocks * EXPERT_BLOCK
    e_safe = jnp.minimum(e_flat, n_exp - 1)
    routed = e_flat < n_exp
    dest = jnp.where(routed, pad_start[e_safe] + rank, n_rows)
    tok = jnp.arange(a, dtype=jnp.int32) // TOP_K
    src_tok = jnp.zeros((n_rows,), jnp.int32).at[dest].set(tok, mode="drop")
    blk_start = jnp.arange(n_blocks, dtype=jnp.int32) * EXPERT_BLOCK
    blk_e = jnp.minimum(jnp.sum(pad_end[None, :] <= blk_start[:, None], axis=1), n_exp - 1).astype(jnp.int32)
    n_act = (pad_end[-1] // EXPERT_BLOCK).astype(jnp.int32).reshape(1)
    last_e = blk_e[jnp.maximum(n_act[0] - 1, 0)]
    blk_e = jnp.where(jnp.arange(n_blocks) < n_act[0], blk_e, last_e)
    pos = jnp.where(routed, dest, 0).astype(jnp.int32)
    return src_tok, blk_e, n_act, pos


def _rope_tables(n_rows, first_valid):
    inv = ROPE_THETA ** (-jnp.arange(0, ROPE_DIM, 2, dtype=F32) / ROPE_DIM)
    pos = (jnp.arange(n_rows, dtype=jnp.int32) - first_valid).astype(F32)
    ang = pos[:, None] * inv[None, :]
    cos, sin = jnp.cos(ang), jnp.sin(ang)
    zeros = jnp.zeros((n_rows, LANES - ROPE_DIM), F32)
    return jnp.concatenate([cos, cos, zeros], axis=1), jnp.concatenate([-sin, sin, zeros], axis=1)


def kernel(x_prompt, x_sample, meta_tokens, mix_norm, w_in, q_norm, w_uq, kv_norm, w_ukv, pool_w, pool_scale, w_out, ffn_norm, dense_w_gate, dense_w_up, dense_w_down, moe_w_router, moe_w_gate, moe_w_up, moe_w_down, final_norm):
    depth, d = mix_norm.shape
    n_meta = meta_tokens.shape[0]
    ql = q_norm.shape[1]
    kvl = kv_norm.shape[1]
    dp = pool_scale.shape[1]
    n_exp = moe_w_router.shape[2]
    da = N_HEADS * V_DIM
    assert d == da + dp and dp % (len(POOL_WINDOWS) * LANES) == 0
    assert ql % LANES == 0 and kvl % LANES == 0 and n_meta <= LANES and n_meta % 8 == 0
    assert w_in.shape[2] == ql + kvl + ROPE_DIM + dp and n_exp <= LANES
    dims = (d, ql, kvl, dp)
    first_valid = ALIGN - n_meta

    groups = [(x_prompt.shape[0], x_prompt.shape[1]), (x_sample.shape[0], x_sample.shape[1])]
    for _, s_len in groups:
        assert s_len % ALIGN == 0
    lead = jnp.concatenate([jnp.zeros((first_valid, d), F32), meta_tokens.astype(F32)], axis=0)
    inputs_2d = [xs.astype(F32).reshape(-1, d) for xs in (x_prompt, x_sample)]
    n_tok = sum(nb * (ALIGN + s_len) for nb, s_len in groups)
    x = None
    valid = jnp.concatenate(
        [jnp.tile(jnp.arange(ALIGN + s_len) >= first_valid, nb) for nb, s_len in groups])

    tables = [_rope_tables(ALIGN + s_len, first_valid) for _, s_len in groups]

    for l in range(depth):
        wi = w_in[l]
        c0, c1, c2 = ql, ql + kvl, ql + kvl + ROPE_DIM
        win = jnp.concatenate(
            [wi[:, :c1], wi[:, c2:], wi[:, c1:c2], jnp.zeros((d, LANES - ROPE_DIM), F32)], axis=1).astype(BF16)
        wq3 = w_uq[l].reshape(ql, N_HEADS, QK_DIM)
        wq = jnp.concatenate([wq3, jnp.zeros((ql, N_HEADS, HEAD_PAD - QK_DIM), F32)], axis=2)
        wq = wq.reshape(ql, N_HEADS * HEAD_PAD).astype(BF16)
        wkv3 = w_ukv[l].reshape(kvl, N_HEADS, NOPE_DIM + V_DIM)
        wk = wkv3[:, :, :NOPE_DIM].reshape(kvl, N_HEADS * NOPE_DIM).astype(BF16)
        wvt = wkv3[:, :, NOPE_DIM:].reshape(kvl, N_HEADS * V_DIM).T.astype(BF16)
        woa = w_out[l][:da].astype(BF16)
        wop = w_out[l][da:].astype(BF16)
        pw = pool_w[l].astype(BF16)

        tile_off = 0
        x_new = None if l == 0 else x
        for (nb, s_len), (cos_t, sin_t), rows_2d in zip(groups, tables, inputs_2d):
            tps = (ALIGN + s_len) // ALIGN
            n_tiles = nb * tps
            src = _RowSource(rows_2d if l == 0 else x_new, lead, tile_off, tps, from_inputs=l == 0)
            q, k, vt, u = _in_proj(src, n_tiles, mix_norm[l][None], win, q_norm[l][None], wq,
                                   kv_norm[l][None], wk, wvt, cos_t, sin_t, dims)
            attn = _attention(q, k, vt, nb, s_len, first_valid)
            x_new = _out_proj(src, x_new, n_tok, attn, u, tile_off, n_tiles, pw, pool_scale[l][None], woa, wop,
                              dims, first_valid, n_meta + s_len)
            tile_off += n_tiles
        x = x_new

        last = l == depth - 1
        if l % 2 == 0:
            i = l // 2
            x = _dense_ffn(x, ffn_norm[l][None], dense_w_gate[i].astype(BF16), dense_w_up[i].astype(BF16),
                           dense_w_down[i].astype(BF16))
            moe = None
        else:
            i = l // 2
            wr = jnp.concatenate([moe_w_router[i], jnp.zeros((d, LANES - n_exp), F32)], axis=1)
            wr_hi = wr.astype(BF16)
            wr_lo = (wr - wr_hi.astype(F32)).astype(BF16)
            top_e, gates = _router(x, ffn_norm[l][None], jnp.concatenate([wr_hi, wr_lo], axis=1), n_exp)
            src_tok, blk_e, n_act, pos = _routing_tables(top_e[:, :TOP_K], valid, n_exp)
            yb = _moe_experts(x, src_tok, ffn_norm[l][None], moe_w_gate[i].astype(BF16), moe_w_up[i].astype(BF16),
                              moe_w_down[i], blk_e, n_act)
            moe = (gates, pos, yb)
            if not last:
                x = _combine(x, gates, pos, yb, final_norm[None], n_tok // ALIGN, lambda i: i, False)
                moe = None

    outs = []
    tile_off = 0
    for (nb, s_len), xs in zip(groups, (x_prompt, x_sample)):
        real = s_len // ALIGN

        def tile_of(i, tile_off=tile_off, real=real):
            return tile_off + (i // real) * (real + 1) + 1 + i % real

        if moe is not None:
            gates, pos, yb = moe
            y = _combine(x, gates, pos, yb, final_norm[None], nb * real, tile_of, True)
        else:
            y = _final_norm(x, final_norm[None], nb * real, tile_of)
        outs.append(y.reshape(nb, s_len, d).astype(xs.dtype))
        tile_off += nb * (real + 1)
    return tuple(outs)
```

```python
import functools

import jax
import jax.numpy as jnp
from jax import lax
from jax.experimental import pallas as pl
from jax.experimental.pallas import tpu as pltpu

N_HEADS = 8
NOPE_DIM = 128
ROPE_DIM = 64
V_DIM = 128
QK_DIM = NOPE_DIM + ROPE_DIM
HEAD_PAD = 256
V_PAD = V_DIM + 16
POOL_WINDOWS = (2, 4, 8, 16)
HALO = 8
TOP_K = 2
EXPERT_BLOCK = 512
ROPE_THETA = 10000.0
RMS_EPS = 1e-6
LOG2E = 1.4426950408889634

LANES = 128
ALIGN = 256
LOOKAHEAD = 2
MAX_Q_UNITS = 5
MAX_Q_UNITS_WHOLE = 17
NEG = -1e30
VMEM_LIMIT = 60 * 1024 * 1024

F32 = jnp.float32
BF16 = jnp.bfloat16


def _rms(x, g):
    ms = jnp.mean(x * x, axis=-1, keepdims=True)
    return x * lax.rsqrt(ms + RMS_EPS) * g


def _dot(a, b):
    return jnp.dot(a, b, preferred_element_type=F32)


def _dot_nt(a, b):
    return lax.dot_general(a, b, (((1,), (1,)), ((), ())), preferred_element_type=F32)


def _params(sem, flags=None):
    return pltpu.CompilerParams(dimension_semantics=sem, vmem_limit_bytes=VMEM_LIMIT, flags=flags)


def _largest_tile(n, candidates):
    for c in candidates:
        if n % c == 0:
            return c
    raise ValueError(f"no tile in {candidates} divides {n}")


class _RowSource:
    def __init__(self, rows, lead, tile_off, tiles_per_seq, from_inputs):
        self.rows, self.lead, self.from_inputs, self.tiles_per_seq = rows, lead, from_inputs, tiles_per_seq
        real = tiles_per_seq - 1
        if from_inputs:
            self.index = lambda i: ((i // tiles_per_seq) * real + jnp.maximum(i % tiles_per_seq - 1, 0), 0)
        else:
            self.index = lambda i: (i + tile_off, 0)

    def specs(self, t, d):
        return [pl.BlockSpec((t, d), self.index), pl.BlockSpec((ALIGN, d), lambda i: (0, 0))]

    def load(self, x_ref, lead_ref):
        if not self.from_inputs:
            return x_ref[...]
        return jnp.where(pl.program_id(0) % self.tiles_per_seq == 0, lead_ref[...], x_ref[...])


def _in_kernel(x_ref, lead_ref, g_ref, win_ref, qn_ref, wq_ref, kvn_ref, wk_ref, wvt_ref, cos_ref, sin_ref,
               q_ref, k_ref, vt_ref, u_ref, *, src, ql, kvl, dp, scale):
    h = _rms(src.load(x_ref, lead_ref), g_ref[...]).astype(BF16)
    z = _dot(h, win_ref[...])
    cq = z[:, :ql]
    ckv = z[:, ql:ql + kvl]
    u_ref[...] = z[:, ql + kvl:ql + kvl + dp]
    kr = z[:, ql + kvl + dp:]
    cqn = _rms(cq, qn_ref[...]).astype(BF16)
    ckvn = _rms(ckv, kvn_ref[...]).astype(BF16)
    q = _dot(cqn, wq_ref[...])
    kn = _dot(ckvn, wk_ref[...])
    vt = _dot_nt(wvt_ref[...], ckvn).astype(BF16)
    ones = jnp.ones((V_PAD - V_DIM, vt.shape[1]), BF16)
    for hh in range(N_HEADS):
        vt_ref[hh * V_PAD:hh * V_PAD + V_DIM, :] = vt[hh * V_DIM:(hh + 1) * V_DIM]
        vt_ref[hh * V_PAD + V_DIM:(hh + 1) * V_PAD, :] = ones

    c = cos_ref[...]
    s = sin_ref[...]
    first_half = lax.broadcasted_iota(jnp.int32, c.shape, 1) < ROPE_DIM // 2

    def rope(v):
        partner = jnp.where(first_half, pltpu.roll(v, LANES - ROPE_DIM // 2, 1),
                            pltpu.roll(v, ROPE_DIM // 2, 1))
        return v * c + partner * s

    kr_r = rope(kr).astype(BF16)
    for hh in range(N_HEADS):
        lo = hh * HEAD_PAD
        q_ref[:, lo:lo + NOPE_DIM] = (q[:, lo:lo + NOPE_DIM] * scale).astype(BF16)
        q_ref[:, lo + NOPE_DIM:lo + HEAD_PAD] = (rope(q[:, lo + NOPE_DIM:lo + HEAD_PAD]) * scale).astype(BF16)
        k_ref[:, lo:lo + NOPE_DIM] = kn[:, hh * NOPE_DIM:(hh + 1) * NOPE_DIM].astype(BF16)
        k_ref[:, lo + NOPE_DIM:lo + HEAD_PAD] = kr_r


def _in_proj(src, n_tiles, g, win, qn, wq, kvn, wk, wvt, cos_t, sin_t, dims):
    d, ql, kvl, dp = dims
    t = ALIGN
    rows = n_tiles * t
    zw = win.shape[1]
    tiles_per_seq = src.tiles_per_seq
    const = lambda i: (0, 0)
    kern = functools.partial(_in_kernel, src=src, ql=ql, kvl=kvl, dp=dp, scale=QK_DIM ** -0.5 * LOG2E)
    return pl.pallas_call(
        kern,
        grid=(n_tiles,),
        in_specs=src.specs(t, d) + [
            pl.BlockSpec((1, d), const),
            pl.BlockSpec((d, zw), const),
            pl.BlockSpec((1, ql), const),
            pl.BlockSpec((ql, N_HEADS * HEAD_PAD), const),
            pl.BlockSpec((1, kvl), const),
            pl.BlockSpec((kvl, N_HEADS * NOPE_DIM), const),
            pl.BlockSpec((N_HEADS * V_DIM, kvl), const),
            pl.BlockSpec((t, LANES), lambda i: (i % tiles_per_seq, 0)),
            pl.BlockSpec((t, LANES), lambda i: (i % tiles_per_seq, 0)),
        ],
        out_specs=[
            pl.BlockSpec((t, N_HEADS * HEAD_PAD), lambda i: (i, 0)),
            pl.BlockSpec((t, N_HEADS * HEAD_PAD), lambda i: (i, 0)),
            pl.BlockSpec((N_HEADS * V_PAD, t), lambda i: (0, i)),
            pl.BlockSpec((t, dp), lambda i: (i, 0)),
        ],
        out_shape=[
            jax.ShapeDtypeStruct((rows, N_HEADS * HEAD_PAD), BF16),
            jax.ShapeDtypeStruct((rows, N_HEADS * HEAD_PAD), BF16),
            jax.ShapeDtypeStruct((N_HEADS * V_PAD, rows), BF16),
            jax.ShapeDtypeStruct((rows, dp), F32),
        ],
        compiler_params=_params(("arbitrary",)),
        name="in_proj",
    )(src.rows, src.lead, g, win, qn, wq, kvn, wk, wvt, cos_t, sin_t)


def _attn_kernel(q_ref, k_ref, vt_ref, o_ref, *, s_len, tk, n_sub, first_valid):
    q = q_ref[...]
    tq = q.shape[0]

    def update(s, vtb, carry):
        m, acc = carry
        m_new = jnp.maximum(m, jnp.max(s, axis=0, keepdims=True))
        p = jnp.exp2(s - m_new).astype(BF16)
        acc = jnp.exp2(m - m_new) * acc + _dot(vtb, p)
        return m_new, acc

    def score(kb):
        return _dot_nt(kb, q)

    carry = (jnp.full((1, tq), NEG, F32), jnp.zeros((V_PAD, tq), F32))
    lo = ALIGN - LANES
    row = lax.broadcasted_iota(jnp.int32, (LANES, tq), 0) + lo
    s_lead = jnp.where(row >= first_valid, _dot_nt(k_ref[lo:ALIGN, :], q), NEG)
    carry = update(s_lead, vt_ref[:, lo:ALIGN], carry)

    def body(c, carry):
        offs = [pl.multiple_of(ALIGN + (c * n_sub + j) * tk, LANES) for j in range(n_sub)]
        scores = [score(k_ref[pl.ds(off, tk), :]) for off in offs[:LOOKAHEAD]]
        for j, off in enumerate(offs):
            if j + LOOKAHEAD < n_sub:
                scores.append(score(k_ref[pl.ds(offs[j + LOOKAHEAD], tk), :]))
            carry = update(scores[j], vt_ref[:, pl.ds(off, tk)], carry)
        return carry

    _, acc = lax.fori_loop(0, s_len // (tk * n_sub), body, carry)
    o_ref[...] = (acc[:V_DIM] / acc[V_DIM:V_DIM + 1]).T.astype(BF16)


def _attention(q, k, vt, n_seq, s_len, first_valid):
    region = ALIGN + s_len
    units = region // ALIGN
    q_units = units if units <= MAX_Q_UNITS_WHOLE else max(u for u in range(1, MAX_Q_UNITS + 1) if units % u == 0)
    tq = q_units * ALIGN
    nq = units // q_units
    tk = _largest_tile(s_len, (512, 256))
    n_sub = _largest_tile(s_len // tk, (8, 4, 2, 1) if q_units <= MAX_Q_UNITS else (2, 1))
    kern = functools.partial(_attn_kernel, s_len=s_len, tk=tk, n_sub=n_sub, first_valid=first_valid)
    return pl.pallas_call(
        kern,
        grid=(n_seq, N_HEADS, nq),
        in_specs=[
            pl.BlockSpec((tq, HEAD_PAD), lambda s, h, i: (s * nq + i, h)),
            pl.BlockSpec((region, HEAD_PAD), lambda s, h, i: (s, h)),
            pl.BlockSpec((V_PAD, region), lambda s, h, i: (h, s)),
        ],
        out_specs=pl.BlockSpec((tq, V_DIM), lambda s, h, i: (s * nq + i, h)),
        out_shape=jax.ShapeDtypeStruct((n_seq * region, N_HEADS * V_DIM), BF16),
        compiler_params=_params(("arbitrary", "arbitrary", "arbitrary")),
        name="attention",
    )(q, k, vt)


def _out_kernel(attn_ref, u_ref, uprev_ref, unext_ref, x_ref, lead_ref, pw_ref, ps_ref, woa_ref, wop_ref, *rest,
                src, first_valid, n_pos):
    o_ref = rest[-1]
    t, dp = u_ref.shape
    pg = dp // len(POOL_WINDOWS)
    p0 = (pl.program_id(0) % src.tiles_per_seq) * t - first_valid
    u = u_ref[...]
    ucat = jnp.concatenate([uprev_ref[...], u, unext_ref[...]], axis=0).astype(BF16)
    tt = lax.broadcasted_iota(jnp.int32, (t, t + 2 * HALO), 0)
    jj = lax.broadcasted_iota(jnp.int32, (t, t + 2 * HALO), 1)
    rel = jj - HALO - tt
    pj = p0 - HALO + jj
    col_ok = (pj >= 0) & (pj < n_pos)
    pos = p0 + lax.broadcasted_iota(jnp.int32, (t, 1), 0)
    ps = ps_ref[...]
    pooled = []
    for gi, w in enumerate(POOL_WINDOWS):
        band = jnp.where((rel >= -(w // 2)) & (rel <= w // 2 - 1) & col_ok, 1.0, 0.0).astype(BF16)
        wsum = _dot(band, ucat[:, gi * pg:(gi + 1) * pg])
        cnt = jnp.minimum(pos + w // 2, n_pos) - jnp.maximum(pos - w // 2, 0)
        cnt = jnp.maximum(cnt, 1).astype(F32)
        centred = (wsum / cnt - u[:, gi * pg:(gi + 1) * pg]).astype(BF16)
        pooled.append(_dot(centred, pw_ref[gi]) * ps[:, gi * pg:(gi + 1) * pg])
    pool = jnp.concatenate(pooled, axis=1).astype(BF16)
    y = _dot(attn_ref[...], woa_ref[...]) + _dot(pool, wop_ref[...])
    o_ref[...] = src.load(x_ref, lead_ref) + y


def _out_proj(src, dst, n_tok, attn, u, tile_off, n_tiles, pw, ps, woa, wop, dims, first_valid, n_pos):
    d, _, _, dp = dims
    t = ALIGN
    pg = dp // len(POOL_WINDOWS)
    da = N_HEADS * V_DIM
    hb = t // HALO
    last = n_tiles * hb - 1
    const = lambda i: (0, 0)
    kern = functools.partial(_out_kernel, src=src, first_valid=first_valid, n_pos=n_pos)
    operands = [attn, u, u, u, src.rows, src.lead, pw, ps, woa, wop]
    in_specs = [
        pl.BlockSpec((t, da), lambda i: (i, 0)),
        pl.BlockSpec((t, dp), lambda i: (i, 0)),
        pl.BlockSpec((HALO, dp), lambda i: (jnp.maximum(i * hb - 1, 0), 0)),
        pl.BlockSpec((HALO, dp), lambda i: (jnp.minimum((i + 1) * hb, last), 0)),
    ] + src.specs(t, d) + [
        pl.BlockSpec((len(POOL_WINDOWS), pg, pg), lambda i: (0, 0, 0)),
        pl.BlockSpec((1, dp), const),
        pl.BlockSpec((da, d), const),
        pl.BlockSpec((dp, d), const),
    ]
    if not src.from_inputs:
        aliases = {4: 0}
    elif dst is None:
        aliases = {}
    else:
        operands.append(dst)
        in_specs.append(pl.BlockSpec(memory_space=pl.ANY))
        aliases = {len(operands) - 1: 0}
    return pl.pallas_call(
        kern,
        grid=(n_tiles,),
        in_specs=in_specs,
        out_specs=pl.BlockSpec((t, d), lambda i: (i + tile_off, 0)),
        out_shape=jax.ShapeDtypeStruct((n_tok, d), F32),
        input_output_aliases=aliases,
        compiler_params=_params(("arbitrary",)),
        name="out_proj",
    )(*operands)


def _swiglu(h, wg, wu, wd):
    a = _dot(h, wg)
    b = _dot(h, wu)
    act = a * (1.0 / (1.0 + jnp.exp(-a))) * b
    return _dot(act.astype(wd.dtype), wd)


CAST_COLS = 1024


def _cast_block_rows(rows, cols, n_steps):
    if cols % CAST_COLS:
        return None
    for b in (512, 1024, 2048):
        if rows % b == 0 and (rows // b) * (cols // CAST_COLS) <= n_steps:
            return b
    return None


def _dense_steps(n, dff):
    return (n // _largest_tile(n, (768, 512, 256))) * (dff // _largest_tile(dff, (512, 256, 128)))


def _ffn_kernel(x_ref, g_ref, wg_ref, wu_ref, wd_ref, *rest, n_cast, n_cast_blocks):
    cast_in, (o_ref, *cast_out), h_ref = rest[:n_cast], rest[n_cast:2 * n_cast + 1], rest[-1]

    @pl.when(pl.program_id(1) == 0)
    def _():
        x = x_ref[...]
        h_ref[...] = _rms(x, g_ref[...]).astype(BF16)
        o_ref[...] = x

    o_ref[...] += _swiglu(h_ref[...], wg_ref[...], wu_ref[...], wd_ref[...])

    if n_cast:
        @pl.when(pl.program_id(0) * pl.num_programs(1) + pl.program_id(1) < n_cast_blocks)
        def _():
            for src_ref, dst_ref in zip(cast_in, cast_out):
                dst_ref[...] = src_ref[...].astype(BF16)


def _dense_ffn(x, g, wg, wu, wd, to_cast=()):
    n, d = x.shape
    dff = wg.shape[1]
    t = _largest_tile(n, (768, 512, 256))
    tf = _largest_tile(dff, (512, 256, 128))
    ni, nj = n // t, dff // tf
    cast_specs, cast_shapes, n_blocks = [], [], 0
    if to_cast:
        rows, cols = to_cast[0].shape
        assert all(a.shape == (rows, cols) for a in to_cast) and cols % CAST_COLS == 0
        nc = cols // CAST_COLS
        br = _cast_block_rows(rows, cols, ni * nj)
        n_blocks = (rows // br) * nc

        def cast_index(i, j):
            s = jnp.minimum(i * nj + j, n_blocks - 1)
            return (s // nc, s % nc)

        cast_specs = [pl.BlockSpec((br, CAST_COLS), cast_index) for _ in to_cast]
        cast_shapes = [jax.ShapeDtypeStruct((rows, cols), BF16) for _ in to_cast]
    out = pl.pallas_call(
        functools.partial(_ffn_kernel, n_cast=len(to_cast), n_cast_blocks=n_blocks),
        grid=(ni, nj),
        in_specs=[
            pl.BlockSpec((t, d), lambda i, j: (i, 0)),
            pl.BlockSpec((1, d), lambda i, j: (0, 0)),
            pl.BlockSpec((d, tf), lambda i, j: (0, j)),
            pl.BlockSpec((d, tf), lambda i, j: (0, j)),
            pl.BlockSpec((tf, d), lambda i, j: (j, 0)),
        ] + cast_specs,
        out_specs=[pl.BlockSpec((t, d), lambda i, j: (i, 0))] + cast_specs,
        out_shape=[jax.ShapeDtypeStruct((n, d), F32)] + cast_shapes,
        scratch_shapes=[pltpu.VMEM((t, d), BF16)],
        compiler_params=_params(("arbitrary", "arbitrary")),
        name="dense_ffn",
    )(x, g, wg, wu, wd, *to_cast)
    return out[0], out[1:]


def _router_kernel(x_ref, g_ref, wr_ref, e_ref, p_ref, *, n_exp):
    h = _rms(x_ref[...], g_ref[...])
    h_hi = h.astype(BF16)
    h_lo = (h - h_hi.astype(F32)).astype(BF16)
    w = wr_ref[...]
    hh = _dot(h_hi, w)
    logits = hh[:, :LANES] + hh[:, LANES:] + _dot(h_lo, w[:, :LANES])
    lane = lax.broadcasted_iota(jnp.int32, logits.shape, 1)
    logits = jnp.where(lane < n_exp, logits, -jnp.inf)
    lane_f = lane.astype(F32)
    m1 = jnp.max(logits, axis=1, keepdims=True)
    i1 = jnp.min(jnp.where(logits == m1, lane_f, float(LANES)), axis=1, keepdims=True)
    rest = jnp.where(lane_f == i1, -jnp.inf, logits)
    m2 = jnp.max(rest, axis=1, keepdims=True)
    i2 = jnp.min(jnp.where(rest == m2, lane_f, float(LANES)), axis=1, keepdims=True)
    e2 = jnp.exp(m2 - m1)
    den = 1.0 + e2
    e_ref[...] = jnp.where(lane == 0, i1, jnp.where(lane == 1, i2, 0.0)).astype(jnp.int32)
    p_ref[...] = jnp.where(lane == 0, 1.0 / den, jnp.where(lane == 1, e2 / den, 0.0))


def _router(x, g, wr, n_exp):
    n, d = x.shape
    t = _largest_tile(n, (768, 512, 256))
    return pl.pallas_call(
        functools.partial(_router_kernel, n_exp=n_exp),
        grid=(n // t,),
        in_specs=[
            pl.BlockSpec((t, d), lambda i: (i, 0)),
            pl.BlockSpec((1, d), lambda i: (0, 0)),
            pl.BlockSpec((d, 2 * LANES), lambda i: (0, 0)),
        ],
        out_specs=[pl.BlockSpec((t, LANES), lambda i: (i, 0)), pl.BlockSpec((t, LANES), lambda i: (i, 0))],
        out_shape=[jax.ShapeDtypeStruct((n, LANES), jnp.int32), jax.ShapeDtypeStruct((n, LANES), F32)],
        compiler_params=_params(("arbitrary",)),
        name="router",
    )(x, g, wr)


DMA_UNROLL = 8


def _row_copies(idx_ref, base, src_hbm, dst_ref, sem, lo, n, wait):
    assert n % 2 == 0

    def pair(r, _):
        for k in range(2):
            row = lo + 2 * r + k
            cp = pltpu.make_async_copy(src_hbm.at[pl.ds(idx_ref[base + row], 1), :], dst_ref.at[pl.ds(row, 1), :], sem)
            if wait:
                cp.wait()
            else:
                cp.start(priority=k)
        return 0

    lax.fori_loop(0, n // 2, pair, 0, unroll=DMA_UNROLL // 2)


def _moe_kernel(src_ref, blk_e_ref, n_act_ref, x_hbm, g_ref, wg_ref, wu_ref, wd_ref, o_ref, xbuf, h_ref, sem, *, nj):
    b = pl.program_id(0)
    j = pl.program_id(1)
    r = o_ref.shape[0]
    n_act = n_act_ref[0]
    slot = b % 2
    per_step = -(-r // (nj * DMA_UNROLL)) * DMA_UNROLL
    last_step = r - (nj - 1) * per_step
    assert last_step > 0

    @pl.when((b == 0) & (j == 0) & (n_act > 0))
    def _():
        _row_copies(src_ref, 0, x_hbm, xbuf.at[0], sem.at[0], 0, r, wait=False)

    @pl.when(j == 0)
    def _():
        o_ref[...] = jnp.zeros_like(o_ref)

        @pl.when(b < n_act)
        def _():
            _row_copies(src_ref, b * r, x_hbm, xbuf.at[slot], sem.at[slot], 0, r, wait=True)
            h_ref[...] = _rms(xbuf[slot], g_ref[...]).astype(BF16)

    def prefetch(n):
        _row_copies(src_ref, (b + 1) * r, x_hbm, xbuf.at[1 - slot], sem.at[1 - slot], j * per_step, n, wait=False)

    if nj > 1:
        pl.when((b + 1 < n_act) & (j < nj - 1))(lambda: prefetch(per_step))
    pl.when((b + 1 < n_act) & (j == nj - 1))(lambda: prefetch(last_step))

    @pl.when(b < n_act)
    def _():
        o_ref[...] += _swiglu(h_ref[...], wg_ref[0], wu_ref[0], wd_ref[0])


def _moe_experts(x, src_tok, g, wg, wu, wd, blk_e, n_act):
    d = x.shape[1]
    n_rows = src_tok.shape[0]
    dff = wg.shape[2]
    r = EXPERT_BLOCK
    tf = _largest_tile(dff, (1024, 512, 256, 128))
    nj = dff // tf

    def wmap_in(b, j, src, blk_e, n_act):
        return (blk_e[b], 0, jnp.where(b < n_act[0], j, nj - 1))

    def wmap_dn(b, j, src, blk_e, n_act):
        return (blk_e[b], jnp.where(b < n_act[0], j, nj - 1), 0)

    return pl.pallas_call(
        functools.partial(_moe_kernel, nj=nj),
        grid_spec=pltpu.PrefetchScalarGridSpec(
            num_scalar_prefetch=3,
            grid=(n_rows // r, nj),
            in_specs=[
                pl.BlockSpec(memory_space=pl.ANY),
                pl.BlockSpec((1, d), lambda b, j, *_: (0, 0)),
                pl.BlockSpec((1, d, tf), wmap_in),
                pl.BlockSpec((1, d, tf), wmap_in),
                pl.BlockSpec((1, tf, d), wmap_dn),
            ],
            out_specs=pl.BlockSpec((r, d), lambda b, j, *_: (b, 0)),
            scratch_shapes=[pltpu.VMEM((2, r, d), F32), pltpu.VMEM((r, d), BF16), pltpu.SemaphoreType.DMA((2,))],
        ),
        out_shape=jax.ShapeDtypeStruct((n_rows, d), F32),
        compiler_params=_params(("arbitrary", "arbitrary")),
        name="moe_experts",
    )(src_tok, blk_e, n_act, x, g, wg, wu, wd)


def _combine_kernel(pos_ref, x_ref, p_ref, fn_ref, y_hbm, o_ref, y_ref, sem, *, tile_of, final_norm):
    t = x_ref.shape[0]
    i = pl.program_id(0)

    def tile_copies(tile, slot, wait):
        base = tile_of(tile) * t * TOP_K

        def one(r, _):
            for k in range(TOP_K):
                cp = pltpu.make_async_copy(y_hbm.at[pl.ds(pos_ref[base + TOP_K * r + k], 1), :],
                                           y_ref.at[slot, k, pl.ds(r, 1), :], sem.at[slot])
                if wait:
                    cp.wait()
                else:
                    cp.start(priority=k)
            return 0

        lax.fori_loop(0, t, one, 0, unroll=DMA_UNROLL // TOP_K)

    slot = i % 2
    pl.when(i == 0)(lambda: tile_copies(i, 0, wait=False))
    pl.when(i + 1 < pl.num_programs(0))(lambda: tile_copies(i + 1, 1 - slot, wait=False))
    tile_copies(i, slot, wait=True)
    p = p_ref[...]
    y = x_ref[...] + (y_ref[slot, 0] * p[:, 0:1] + y_ref[slot, 1] * p[:, 1:2])
    if final_norm:
        y = _rms(y, fn_ref[...])
    o_ref[...] = y


def _combine(x, gates, pos, yb, fn, n_out_tiles, tile_of, final_norm):
    d = x.shape[1]
    t = ALIGN
    kern = functools.partial(_combine_kernel, tile_of=tile_of, final_norm=final_norm)
    return pl.pallas_call(
        kern,
        grid_spec=pltpu.PrefetchScalarGridSpec(
            num_scalar_prefetch=1,
            grid=(n_out_tiles,),
            in_specs=[
                pl.BlockSpec((t, d), lambda i, pos: (tile_of(i), 0)),
                pl.BlockSpec((t, LANES), lambda i, pos: (tile_of(i), 0)),
                pl.BlockSpec((1, d), lambda i, pos: (0, 0)),
                pl.BlockSpec(memory_space=pl.ANY),
            ],
            out_specs=pl.BlockSpec((t, d), lambda i, pos: (i, 0)),
            scratch_shapes=[pltpu.VMEM((2, TOP_K, t, d), F32), pltpu.SemaphoreType.DMA((2,))],
        ),
        out_shape=jax.ShapeDtypeStruct((n_out_tiles * t, d), F32),
        compiler_params=_params(("arbitrary",)),
        name="moe_combine",
    )(pos, x, gates, fn, yb)


def _norm_kernel(x_ref, g_ref, o_ref):
    o_ref[...] = _rms(x_ref[...], g_ref[...])


def _final_norm(x, g, n_out_tiles, tile_of):
    d = x.shape[1]
    t = ALIGN
    return pl.pallas_call(
        _norm_kernel,
        grid=(n_out_tiles,),
        in_specs=[pl.BlockSpec((t, d), lambda i: (tile_of(i), 0)), pl.BlockSpec((1, d), lambda i: (0, 0))],
        out_specs=pl.BlockSpec((t, d), lambda i: (i, 0)),
        out_shape=jax.ShapeDtypeStruct((n_out_tiles * t, d), F32),
        compiler_params=_params(("arbitrary",)),
        name="final_norm",
    )(x, g)


def _routing_tables(top_e, valid, n_exp):
    n = top_e.shape[0]
    a = n * TOP_K
    e_flat = jnp.where(valid[:, None], top_e, n_exp).reshape(-1)
    onehot = (e_flat[:, None] == jnp.arange(n_exp, dtype=jnp.int32)[None, :]).astype(jnp.int32)
    csum = jnp.cumsum(onehot, axis=0)
    counts = csum[-1]
    rank = jnp.sum((csum - onehot) * onehot, axis=1)
    padded = (counts + EXPERT_BLOCK - 1) // EXPERT_BLOCK * EXPERT_BLOCK
    pad_end = jnp.cumsum(padded)
    pad_start = pad_end - padded
    n_blocks = -(-a // EXPERT_BLOCK) + n_exp
    n_rows = n_blocks * EXPERT_BLOCK
    e_safe = jnp.minimum(e_flat, n_exp - 1)
    routed = e_flat < n_exp
    dest = jnp.where(routed, pad_start[e_safe] + rank, n_rows)
    tok = jnp.arange(a, dtype=jnp.int32) // TOP_K
    src_tok = jnp.zeros((n_rows,), jnp.int32).at[dest].set(tok, mode="drop")
    blk_start = jnp.arange(n_blocks, dtype=jnp.int32) * EXPERT_BLOCK
    blk_e = jnp.minimum(jnp.sum(pad_end[None, :] <= blk_start[:, None], axis=1), n_exp - 1).astype(jnp.int32)
    n_act = (pad_end[-1] // EXPERT_BLOCK).astype(jnp.int32).reshape(1)
    last_e = blk_e[jnp.maximum(n_act[0] - 1, 0)]
    blk_e = jnp.where(jnp.arange(n_blocks) < n_act[0], blk_e, last_e)
    pos = jnp.where(routed, dest, 0).astype(jnp.int32)
    return src_tok, blk_e, n_act, pos


def _rope_tables(n_rows, first_valid):
    inv = ROPE_THETA ** (-jnp.arange(0, ROPE_DIM, 2, dtype=F32) / ROPE_DIM)
    pos = (jnp.arange(n_rows, dtype=jnp.int32) - first_valid).astype(F32)
    ang = pos[:, None] * inv[None, :]
    cos, sin = jnp.cos(ang), jnp.sin(ang)
    zeros = jnp.zeros((n_rows, LANES - ROPE_DIM), F32)
    return jnp.concatenate([cos, cos, zeros], axis=1), jnp.concatenate([-sin, sin, zeros], axis=1)


def kernel(x_prompt, x_sample, meta_tokens, mix_norm, w_in, q_norm, w_uq, kv_norm, w_ukv, pool_w, pool_scale, w_out, ffn_norm, dense_w_gate, dense_w_up, dense_w_down, moe_w_router, moe_w_gate, moe_w_up, moe_w_down, final_norm):
    depth, d = mix_norm.shape
    n_meta = meta_tokens.shape[0]
    ql = q_norm.shape[1]
    kvl = kv_norm.shape[1]
    dp = pool_scale.shape[1]
    n_exp = moe_w_router.shape[2]
    da = N_HEADS * V_DIM
    assert d == da + dp and dp % (len(POOL_WINDOWS) * LANES) == 0
    assert ql % LANES == 0 and kvl % LANES == 0 and n_meta <= LANES and n_meta % 8 == 0
    assert w_in.shape[2] == ql + kvl + ROPE_DIM + dp and n_exp <= LANES
    dims = (d, ql, kvl, dp)
    first_valid = ALIGN - n_meta

    groups = [(x_prompt.shape[0], x_prompt.shape[1]), (x_sample.shape[0], x_sample.shape[1])]
    for _, s_len in groups:
        assert s_len % ALIGN == 0
    lead = jnp.concatenate([jnp.zeros((first_valid, d), F32), meta_tokens.astype(F32)], axis=0)
    inputs_2d = [xs.astype(F32).reshape(-1, d) for xs in (x_prompt, x_sample)]
    n_tok = sum(nb * (ALIGN + s_len) for nb, s_len in groups)
    x = None
    moe_gu = []
    valid = jnp.concatenate(
        [jnp.tile(jnp.arange(ALIGN + s_len) >= first_valid, nb) for nb, s_len in groups])

    tables = [_rope_tables(ALIGN + s_len, first_valid) for _, s_len in groups]

    for l in range(depth):
        wi = w_in[l]
        c0, c1, c2 = ql, ql + kvl, ql + kvl + ROPE_DIM
        win = jnp.concatenate(
            [wi[:, :c1], wi[:, c2:], wi[:, c1:c2], jnp.zeros((d, LANES - ROPE_DIM), F32)], axis=1).astype(BF16)
        wq3 = w_uq[l].reshape(ql, N_HEADS, QK_DIM)
        wq = jnp.concatenate([wq3, jnp.zeros((ql, N_HEADS, HEAD_PAD - QK_DIM), F32)], axis=2)
        wq = wq.reshape(ql, N_HEADS * HEAD_PAD).astype(BF16)
        wkv3 = w_ukv[l].reshape(kvl, N_HEADS, NOPE_DIM + V_DIM)
        wk = wkv3[:, :, :NOPE_DIM].reshape(kvl, N_HEADS * NOPE_DIM).astype(BF16)
        wvt = wkv3[:, :, NOPE_DIM:].reshape(kvl, N_HEADS * V_DIM).T.astype(BF16)
        woa = w_out[l][:da].astype(BF16)
        wop = w_out[l][da:].astype(BF16)
        pw = pool_w[l].astype(BF16)

        tile_off = 0
        x_new = None if l == 0 else x
        for (nb, s_len), (cos_t, sin_t), rows_2d in zip(groups, tables, inputs_2d):
            tps = (ALIGN + s_len) // ALIGN
            n_tiles = nb * tps
            src = _RowSource(rows_2d if l == 0 else x_new, lead, tile_off, tps, from_inputs=l == 0)
            q, k, vt, u = _in_proj(src, n_tiles, mix_norm[l][None], win, q_norm[l][None], wq,
                                   kv_norm[l][None], wk, wvt, cos_t, sin_t, dims)
            attn = _attention(q, k, vt, nb, s_len, first_valid)
            x_new = _out_proj(src, x_new, n_tok, attn, u, tile_off, n_tiles, pw, pool_scale[l][None], woa, wop,
                              dims, first_valid, n_meta + s_len)
            tile_off += n_tiles
        x = x_new

        last = l == depth - 1
        if l % 2 == 0:
            i = l // 2
            to_cast = ()
            if l + 1 < depth:
                gu = [w[(l + 1) // 2].reshape(n_exp * d, -1) for w in (moe_w_gate, moe_w_up)]
                if _cast_block_rows(*gu[0].shape, _dense_steps(n_tok, dense_w_gate.shape[2])) is not None:
                    to_cast = tuple(gu)
            x, cast = _dense_ffn(x, ffn_norm[l][None], dense_w_gate[i].astype(BF16), dense_w_up[i].astype(BF16),
                                 dense_w_down[i].astype(BF16), to_cast)
            moe_gu = [c.reshape(n_exp, d, -1) for c in cast]
            moe = None
        else:
            i = l // 2
            if not moe_gu:
                moe_gu = [moe_w_gate[i].astype(BF16), moe_w_up[i].astype(BF16)]
            wr = jnp.concatenate([moe_w_router[i], jnp.zeros((d, LANES - n_exp), F32)], axis=1)
            wr_hi = wr.astype(BF16)
            wr_lo = (wr - wr_hi.astype(F32)).astype(BF16)
            top_e, gates = _router(x, ffn_norm[l][None], jnp.concatenate([wr_hi, wr_lo], axis=1), n_exp)
            src_tok, blk_e, n_act, pos = _routing_tables(top_e[:, :TOP_K], valid, n_exp)
            yb = _moe_experts(x, src_tok, ffn_norm[l][None], moe_gu[0], moe_gu[1], moe_w_down[i], blk_e, n_act)
            moe_gu = []
            moe = (gates, pos, yb)
            if not last:
                x = _combine(x, gates, pos, yb, final_norm[None], n_tok // ALIGN, lambda i: i, False)
                moe = None

    outs = []
    tile_off = 0
    for (nb, s_len), xs in zip(groups, (x_prompt, x_sample)):
        real = s_len // ALIGN

        def tile_of(i, tile_off=tile_off, real=real):
            return tile_off + (i // real) * (real + 1) + 1 + i % real

        if moe is not None:
            gates, pos, yb = moe
            y = _combine(x, gates, pos, yb, final_norm[None], nb * real, tile_of, True)
        else:
            y = _final_norm(x, final_norm[None], nb * real, tile_of)
        outs.append(y.reshape(nb, s_len, d).astype(xs.dtype))
        tile_off += nb * (real + 1)
    return tuple(outs)
```
